```python
import math
import jax, jax.numpy as jnp
from jax import lax
import numpy as np

D_MODEL = 2048
BATCH = 4
SEQ = 8192
DEPTH = 4
DEC_BATCH = 4
DEC_SEQ = 2048
PAST_LEN = 128

D_PLE = 256
EPS = 1e-6
POOL_WINDOWS = (2, 4, 8, 16)
POOL_GROUPS = 4
POOL_CH = D_MODEL // 16
D_POOL = POOL_GROUPS * POOL_CH
DA_HEADS = 4
DA_DH = 64
DA_DV = 2 * DA_DH
D_DA = DA_HEADS * DA_DV
Q_BLOCK = 128
ROPE_THETA = 10000.0
SSD_HEADS = 16
SSD_HEAD_DIM = 64
D_SSD = SSD_HEADS * SSD_HEAD_DIM
SSD_GROUPS = 2
SSD_STATE = 128
CONV_K = 5
CHUNK = 128
D_XBC = D_SSD + 2 * SSD_GROUPS * SSD_STATE
D_MIX = D_POOL + D_DA + D_SSD
OFF_Q = D_POOL
OFF_K = OFF_Q + D_DA
OFF_V = OFF_K + D_DA
OFF_Z = OFF_V + D_DA
OFF_XBC = OFF_Z + D_SSD
OFF_DT = OFF_XBC + D_XBC
D_IN = OFF_DT + 2 * SSD_HEADS
N_GROUPS = 4
EXP_PER_GROUP = 8
N_EXPERTS = N_GROUPS * EXP_PER_GROUP
D_EXPERT = 512
TOP_K = 2
MOE_BLOCK = 128

kernel_name = "hybrid_bidir_pool_diffattn_ssd_hmoe"

F32 = jnp.float32


def rms_norm(x, g):
    xf = x.astype(F32)
    y = xf * lax.rsqrt(jnp.mean(xf * xf, axis=-1, keepdims=True) + EPS)
    return (y * g.astype(F32)).astype(x.dtype)


def rope(x, pos):
    d = x.shape[-1]
    half = d // 2
    inv = jnp.power(ROPE_THETA, -jnp.arange(half, dtype=F32) * 2.0 / d)
    ang = pos[:, None] * inv[None, :]
    cos = jnp.cos(ang)[None, :, None, None, :]
    sin = jnp.sin(ang)[None, :, None, None, :]
    xf = x.astype(F32)
    x1, x2 = xf[..., :half], xf[..., half:]
    return jnp.concatenate([x1 * cos - x2 * sin, x2 * cos + x1 * sin], axis=-1).astype(x.dtype)


def pool_mixer(u, pool_w, pool_scale):
    b, s, _ = u.shape
    ug = u.reshape(b, s, POOL_GROUPS, POOL_CH).astype(F32)
    csum = jnp.concatenate([jnp.zeros((b, 1, POOL_GROUPS, POOL_CH), F32), jnp.cumsum(ug, axis=1)], axis=1)
    t = jnp.arange(s)
    outs = []
    for gi, win in enumerate(POOL_WINDOWS):
        left = win // 2
        right = win - 1 - left
        lo = jnp.clip(t - left, 0, s)
        hi = jnp.clip(t + right + 1, 0, s)
        cnt = (hi - lo).astype(F32)
        mean = (csum[:, hi, gi] - csum[:, lo, gi]) / cnt[None, :, None]
        outs.append(mean - ug[:, :, gi])
    dlt = jnp.stack(outs, axis=2)
    y = jnp.einsum('bsgc,gce->bsge', dlt, pool_w.astype(F32))
    return (y.reshape(b, s, D_POOL) * pool_scale.astype(F32)).astype(u.dtype)


def diff_attention(q, k, v, q_norm_g, k_norm_g, lam_qk, subln_g, lam_init):
    b, s, _ = q.shape
    q = q.reshape(b, s, DA_HEADS, 2, DA_DH)
    k = k.reshape(b, s, DA_HEADS, 2, DA_DH)
    v = v.reshape(b, s, DA_HEADS, DA_DV)
    pos = jnp.arange(s, dtype=F32)
    q = rope(rms_norm(q, q_norm_g), pos) * (DA_DH ** -0.5)
    k = rope(rms_norm(k, k_norm_g), pos)
    lf = lam_qk.astype(F32)
    lam = jnp.exp(jnp.sum(lf[0] * lf[1])) - jnp.exp(jnp.sum(lf[2] * lf[3])) + lam_init
    nb = s // Q_BLOCK
    qb = q.reshape(b, nb, Q_BLOCK, DA_HEADS, 2, DA_DH).transpose(1, 0, 2, 3, 4, 5)

    def block(qi):
        sc = jnp.einsum('bqhcd,bkhcd->bhcqk', qi, k, preferred_element_type=F32)
        pr = jax.nn.softmax(sc, axis=-1)
        pd = pr[:, :, 0] - lam * pr[:, :, 1]
        return jnp.einsum('bhqk,bkhe->bqhe', pd.astype(v.dtype), v)

    o = lax.map(block, qb)
    o = o.transpose(1, 0, 2, 3, 4).reshape(b, s, DA_HEADS, DA_DV)
    o = rms_norm(o, subln_g) * (1.0 - lam_init)
    return o.reshape(b, s, D_DA)


def ssd_scan(x, dt, A, Bm, Cm):
    b, l, h, p = x.shape
    g, n = Bm.shape[2], Bm.shape[3]
    hg = h // g
    c = l // CHUNK
    xr = (x * dt[..., None]).reshape(b, c, CHUNK, g, hg, p)
    a = (dt * A).reshape(b, c, CHUNK, g, hg).transpose(0, 1, 3, 4, 2)
    acs = jnp.cumsum(a, axis=-1)
    Br = Bm.reshape(b, c, CHUNK, g, n)
    Cr = Cm.reshape(b, c, CHUNK, g, n)
    diff = acs[..., :, None] - acs[..., None, :]
    mask = jnp.tril(jnp.ones((CHUNK, CHUNK), dtype=bool))
    L = jnp.exp(jnp.where(mask, diff, -jnp.inf))
    CB = jnp.einsum('bcign,bcjgn->bcgij', Cr, Br)
    y_diag = jnp.einsum('bcgij,bcghij,bcjghp->bcighp', CB, L, xr)
    decay = jnp.exp(acs[..., -1:] - acs)
    states = jnp.einsum('bcjgn,bcghj,bcjghp->bcghpn', Br, decay, xr)
    chunk_dec = jnp.exp(acs[..., -1])

    def step(carry, inp):
        st, dc = inp
        return carry * dc[..., None, None] + st, carry

    init = jnp.zeros((b, g, hg, p, n), F32)
    _, prev = lax.scan(step, init, (jnp.moveaxis(states, 1, 0), jnp.moveaxis(chunk_dec, 1, 0)))
    prev = jnp.moveaxis(prev, 0, 1)
    y_off = jnp.einsum('bcign,bcghpn,bcghi->bcighp', Cr, prev, jnp.exp(acs))
    return (y_diag + y_off).reshape(b, l, h, p)


def ssd_mixer(z, xbc, dt_raw, conv_w, conv_b, a_log, dt_bias, d_skip, ssd_norm_g):
    b, s, _ = z.shape
    xbc = lax.conv_general_dilated(xbc, conv_w[:, None, :], window_strides=(1,),
                                   padding=[(CONV_K // 2, CONV_K // 2)],
                                   dimension_numbers=('NWC', 'WIO', 'NWC'),
                                   feature_group_count=D_XBC) + conv_b
    xbc = jax.nn.silu(xbc).astype(F32)
    xs = xbc[..., :D_SSD].reshape(b, s, SSD_HEADS, SSD_HEAD_DIM)
    gn = SSD_GROUPS * SSD_STATE
    Bm = xbc[..., D_SSD:D_SSD + gn].reshape(b, s, SSD_GROUPS, SSD_STATE)
    Cm = xbc[..., D_SSD + gn:].reshape(b, s, SSD_GROUPS, SSD_STATE)
    dt = jax.nn.softplus(dt_raw.astype(F32).reshape(b, s, 2, SSD_HEADS) + dt_bias.astype(F32))
    A = -jnp.exp(a_log.astype(F32))
    y_f = ssd_scan(xs, dt[:, :, 0], A[0], Bm, Cm)
    y_b = ssd_scan(xs[:, ::-1], dt[:, ::-1, 1], A[1], Bm[:, ::-1], Cm[:, ::-1])[:, ::-1]
    y = y_f + y_b + xs * d_skip.astype(F32)[:, None]
    y = y.reshape(b, s, D_SSD) * jax.nn.silu(z.astype(F32))
    return rms_norm(y, ssd_norm_g).astype(z.dtype)


def hier_moe(h, rc_w, rc_b, rf_w, rf_b, w_gu, w_dn):
    n_tok = h.shape[0]
    coarse_p = jax.nn.softmax((h @ rc_w).astype(F32) + rc_b.astype(F32), axis=-1)
    gp, grp = lax.top_k(coarse_p, 1)
    fine = ((h @ rf_w).astype(F32) + rf_b.astype(F32)).reshape(n_tok, N_GROUPS, EXP_PER_GROUP)
    fine_sel = fine[jnp.arange(n_tok), grp[:, 0]]
    fv, fi = lax.top_k(jax.nn.softmax(fine_sel, axis=-1), TOP_K)
    w_tok = gp * fv / jnp.sum(fv, axis=-1, keepdims=True)
    e_tok = grp * EXP_PER_GROUP + fi
    n_asg = n_tok * TOP_K
    e_a = e_tok.reshape(-1)
    w_a = w_tok.reshape(-1)
    t_a = jnp.repeat(jnp.arange(n_tok, dtype=jnp.int32), TOP_K)
    order = jnp.argsort(e_a)
    es, ts, ws = e_a[order], t_a[order], w_a[order]
    counts = jnp.bincount(e_a, length=N_EXPERTS)
    padded = (counts + MOE_BLOCK - 1) // MOE_BLOCK * MOE_BLOCK
    pend = jnp.cumsum(padded)
    pstart = pend - padded
    ustart = jnp.cumsum(counts) - counts
    dest = pstart[es] + jnp.arange(n_asg) - ustart[es]
    n_blocks = -(-n_asg // MOE_BLOCK) + N_EXPERTS
    n_slots = n_blocks * MOE_BLOCK
    slot_tok = jnp.zeros((n_slots,), jnp.int32).at[dest].set(ts)
    slot_w = jnp.zeros((n_slots,), F32).at[dest].set(ws)
    blk_exp = jnp.minimum(jnp.searchsorted(pend, jnp.arange(n_blocks) * MOE_BLOCK, side='right'), N_EXPERTS - 1)

    def run_block(args):
        tok, e = args
        xb = h[tok]
        gu = xb @ w_gu[e]
        g, u = jnp.split(gu, 2, axis=-1)
        return (jax.nn.silu(g) * u) @ w_dn[e]

    out = lax.map(run_block, (slot_tok.reshape(n_blocks, MOE_BLOCK), blk_exp))
    y = jnp.zeros((n_tok, h.shape[-1]), F32).at[slot_tok].add(
        out.reshape(n_slots, -1).astype(F32) * slot_w[:, None])
    return y.astype(h.dtype)


def hybrid_layer(x, pe, layer_idx, w_in, w_out, pool_w, pool_scale, q_norm_g, k_norm_g, lam_qk, subln_g,
                 conv_w, conv_b, a_log, dt_bias, d_skip, ssd_norm_g, norm_mix_g, norm_ffn_g,
                 rc_w, rc_b, rf_w, rf_b, w_gu, w_dn, norm_ple_g, w_ple, w_pg, b_pg):
    b, s, d = x.shape
    lam_init = 0.8 - 0.6 * math.exp(-0.3 * layer_idx)
    h = rms_norm(x, norm_mix_g)
    u = h @ w_in
    y_pool = pool_mixer(u[..., :OFF_Q], pool_w, pool_scale)
    y_da = diff_attention(u[..., OFF_Q:OFF_K], u[..., OFF_K:OFF_V], u[..., OFF_V:OFF_Z],
                          q_norm_g, k_norm_g, lam_qk, subln_g, lam_init)
    y_ssd = ssd_mixer(u[..., OFF_Z:OFF_XBC], u[..., OFF_XBC:OFF_DT], u[..., OFF_DT:],
                      conv_w, conv_b, a_log, dt_bias, d_skip, ssd_norm_g)
    x = x + jnp.concatenate([y_pool, y_da, y_ssd], axis=-1) @ w_out
    hf = rms_norm(x, norm_ffn_g).reshape(b * s, d)
    x = x + hier_moe(hf, rc_w, rc_b, rf_w, rf_b, w_gu, w_dn).reshape(b, s, d)
    gate = jax.nn.sigmoid(rms_norm(x, norm_ple_g) @ w_pg + b_pg)
    return x + gate * (pe @ w_ple)


def setup_inputs(seed: int = 0) -> dict:
    key = jax.random.key(seed)
    ks = jax.random.split(key, 32)

    def nrm(k, shape, scale):
        return jax.random.normal(k, shape, F32) * scale

    dt0 = jnp.exp(jax.random.uniform(ks[14], (DEPTH, 2, SSD_HEADS), F32, math.log(1e-3), math.log(1e-1)))
    return {
        "x_prompt": nrm(ks[0], (BATCH, SEQ, D_MODEL), 1.0),
        "x_sample": nrm(ks[1], (DEC_BATCH, DEC_SEQ, D_MODEL), 1.0),
        "p_prompt": nrm(ks[2], (DEPTH, BATCH, SEQ, D_PLE), 1.0),
        "p_sample": nrm(ks[3], (DEPTH, DEC_BATCH, DEC_SEQ, D_PLE), 1.0),
        "w_in": nrm(ks[4], (DEPTH, D_MODEL, D_IN), D_MODEL ** -0.5),
        "w_out": nrm(ks[5], (DEPTH, D_MIX, D_MODEL), D_MIX ** -0.5),
        "pool_w": nrm(ks[6], (DEPTH, POOL_GROUPS, POOL_CH, POOL_CH), POOL_CH ** -0.5),
        "pool_scale": 1.0 + nrm(ks[7], (DEPTH, D_POOL), 0.02),
        "q_norm_g": 1.0 + nrm(ks[8], (DEPTH, DA_DH), 0.02),
        "k_norm_g": 1.0 + nrm(ks[9], (DEPTH, DA_DH), 0.02),
        "lam_qk": nrm(ks[10], (DEPTH, 4, DA_DH), 0.1),
        "subln_g": 1.0 + nrm(ks[11], (DEPTH, DA_DV), 0.02),
        "conv_w": nrm(ks[12], (DEPTH, CONV_K, D_XBC), CONV_K ** -0.5),
        "conv_b": nrm(ks[13], (DEPTH, D_XBC), 0.02),
        "a_log": jnp.log(jax.random.uniform(ks[15], (DEPTH, 2, SSD_HEADS), F32, 1.0, 16.0)),
        "dt_bias": dt0 + jnp.log(-jnp.expm1(-dt0)),
        "d_skip": 1.0 + nrm(ks[16], (DEPTH, SSD_HEADS), 0.02),
        "ssd_norm_g": 1.0 + nrm(ks[17], (DEPTH, D_SSD), 0.02),
        "norm_mix_g": 1.0 + nrm(ks[18], (DEPTH, D_MODEL), 0.02),
        "norm_ffn_g": 1.0 + nrm(ks[19], (DEPTH, D_MODEL), 0.02),
        "router_coarse_w": nrm(ks[20], (DEPTH, D_MODEL, N_GROUPS), D_MODEL ** -0.5),
        "router_coarse_b": nrm(ks[21], (DEPTH, N_GROUPS), 0.01),
        "router_fine_w": nrm(ks[22], (DEPTH, D_MODEL, N_EXPERTS), D_MODEL ** -0.5),
        "router_fine_b": nrm(ks[23], (DEPTH, N_EXPERTS), 0.01),
        "w_gate_up": nrm(ks[24], (DEPTH, N_EXPERTS, D_MODEL, 2 * D_EXPERT), D_MODEL ** -0.5),
        "w_down": nrm(ks[25], (DEPTH, N_EXPERTS, D_EXPERT, D_MODEL), D_EXPERT ** -0.5),
        "norm_ple_g": 1.0 + nrm(ks[26], (DEPTH, D_MODEL), 0.02),
        "w_ple": nrm(ks[27], (DEPTH, D_PLE, D_MODEL), D_PLE ** -0.5),
        "w_ple_gate": nrm(ks[28], (DEPTH, D_MODEL, D_MODEL), D_MODEL ** -0.5),
        "b_ple_gate": nrm(ks[29], (DEPTH, D_MODEL), 0.02),
    }


def reference(x_prompt, x_sample, p_prompt, p_sample, w_in, w_out, pool_w, pool_scale, q_norm_g, k_norm_g,
              lam_qk, subln_g, conv_w, conv_b, a_log, dt_bias, d_skip, ssd_norm_g, norm_mix_g, norm_ffn_g,
              router_coarse_w, router_coarse_b, router_fine_w, router_fine_b, w_gate_up, w_down,
              norm_ple_g, w_ple, w_ple_gate, b_ple_gate):
    def trunk(x, p):
        for i in range(DEPTH):
            x = hybrid_layer(x, p[i], i, w_in[i], w_out[i], pool_w[i], pool_scale[i], q_norm_g[i], k_norm_g[i],
                             lam_qk[i], subln_g[i], conv_w[i], conv_b[i], a_log[i], dt_bias[i], d_skip[i],
                             ssd_norm_g[i], norm_mix_g[i], norm_ffn_g[i], router_coarse_w[i], router_coarse_b[i],
                             router_fine_w[i], router_fine_b[i], w_gate_up[i], w_down[i], norm_ple_g[i],
                             w_ple[i], w_ple_gate[i], b_ple_gate[i])
        return x

    y_prompt = trunk(x_prompt, p_prompt)
    y_sample = trunk(x_sample, p_sample)
    return (y_prompt, y_sample)
```

```python
import functools
import math

import jax
import jax.numpy as jnp
from jax import lax
from jax.experimental import pallas as pl
from jax.experimental.pallas import tpu as pltpu

F32 = jnp.float32
BF16 = jnp.bfloat16
I32 = jnp.int32
U32 = jnp.uint32

EPS = 1e-6
D_MODEL = 2048
D_PLE = 256
POOL_WINDOWS = (2, 4, 8, 16)
POOL_CH = 128
D_POOL = 512
DA_HEADS = 4
DA_DH = 64
DA_DV = 128
D_DA = 512
ROPE_THETA = 10000.0
SSD_HEADS = 16
SSD_HEAD_DIM = 64
D_SSD = 1024
SSD_GROUPS = 2
SSD_STATE = 128
CONV_K = 5
CHUNK = 128
D_XBC = 1536
N_GROUPS = 4
EXP_PER_GROUP = 8
N_EXPERTS = 32
D_EXPERT = 512

LANES = 128
HALO = 16
VMEM_LIMIT = 56 * 2**20


def _cparams(sem):
    return pltpu.CompilerParams(dimension_semantics=sem, vmem_limit_bytes=VMEM_LIMIT)


def _const_spec(shape):
    nd = len(shape)
    return pl.BlockSpec(shape, lambda *_: (0,) * nd, pipeline_mode=pl.Buffered(1))


def _dot(a, b):
    return jnp.dot(a, b, preferred_element_type=F32)


def _dot_nt(a, b):
    return lax.dot_general(a, b, (((1,), (1,)), ((), ())), preferred_element_type=F32)


def _split2(x):
    hi = x.astype(BF16)
    lo = (x - hi.astype(F32)).astype(BF16)
    return hi, lo


def _split3(x):
    a1 = x.astype(BF16)
    r1 = x - a1.astype(F32)
    a2 = r1.astype(BF16)
    a3 = (r1 - a2.astype(F32)).astype(BF16)
    return a1, a2, a3


def _silu(x):
    return x * jax.nn.sigmoid(x)


def _rms(x, g):
    ms = jnp.mean(x * x, axis=-1, keepdims=True)
    return x * lax.rsqrt(ms + EPS) * g


def _qk_prep(x, g, cos, sin, bd, first_half):
    outs = []
    for c in range(D_DA // LANES):
        xc = x[:, c * LANES:(c + 1) * LANES]
        hi, lo = _split2(xc * xc)
        ss = _dot(hi, bd) + _dot(lo, bd)
        xn = xc * lax.rsqrt(ss * (1.0 / DA_DH) + EPS) * g[:, c * LANES:(c + 1) * LANES]
        sw = jnp.where(first_half, pltpu.roll(xn, LANES - 32, 1), pltpu.roll(xn, 32, 1))
        outs.append((xn * cos + sw * sin).astype(BF16))
    return jnp.concatenate(outs, axis=1)


def _inproj_kernel(x_ref, g_ref, wp_ref, wq_ref, wk_ref, wv_ref, wz_ref, wx_ref, wd_ref,
                   gq_ref, gk_ref, cos_ref, sin_ref,
                   pool_o, q_o, k_o, v_o, z_o, xbc_o, dt_o):
    h = _rms(x_ref[...], g_ref[...]).astype(BF16)
    pool_o[...] = _dot(h, wp_ref[...])
    v_o[...] = _dot(h, wv_ref[...]).astype(BF16)
    z_o[...] = _dot(h, wz_ref[...]).astype(BF16)
    xbc_o[...] = _dot(h, wx_ref[...]).astype(BF16)
    dt_o[...] = _dot(h, wd_ref[...])
    r = lax.broadcasted_iota(I32, (LANES, LANES), 0) // DA_DH
    c = lax.broadcasted_iota(I32, (LANES, LANES), 1) // DA_DH
    bd = jnp.where(r == c, 1.0, 0.0).astype(BF16)
    first_half = (lax.broadcasted_iota(I32, (1, LANES), 1) % DA_DH) < (DA_DH // 2)
    cos = cos_ref[...]
    sin = sin_ref[...]
    q_o[...] = _qk_prep(_dot(h, wq_ref[...]), gq_ref[...], cos, sin, bd, first_half)
    k_o[...] = _qk_prep(_dot(h, wk_ref[...]), gk_ref[...], cos, sin, bd, first_half)


def _inproj(x, lw, cos, sin, seq):
    n = x.shape[0]
    tm = min(512, seq)
    nseq = seq // tm
    row = lambda w: pl.BlockSpec((tm, w), lambda i: (i, 0))
    pos = pl.BlockSpec((tm, LANES), lambda i: (i % nseq, 0))
    ws = [lw["w_pool"], lw["w_q"], lw["w_k"], lw["w_v"], lw["w_z"], lw["w_xbc"], lw["w_dt"]]
    return pl.pallas_call(
        _inproj_kernel,
        grid=(n // tm,),
        in_specs=[row(D_MODEL), _const_spec((1, D_MODEL))] + [_const_spec(w.shape) for w in ws]
        + [_const_spec((1, D_DA)), _const_spec((1, D_DA)), pos, pos],
        out_specs=[row(D_POOL), row(D_DA), row(D_DA), row(D_DA), row(D_SSD), row(D_XBC), row(LANES)],
        out_shape=[jax.ShapeDtypeStruct((n, D_POOL), F32), jax.ShapeDtypeStruct((n, D_DA), BF16),
                   jax.ShapeDtypeStruct((n, D_DA), BF16), jax.ShapeDtypeStruct((n, D_DA), BF16),
                   jax.ShapeDtypeStruct((n, D_SSD), BF16), jax.ShapeDtypeStruct((n, D_XBC), BF16),
                   jax.ShapeDtypeStruct((n, LANES), F32)],
        compiler_params=_cparams(("arbitrary",)),
        name="inproj",
    )(x, lw["g_mix"], *ws, lw["gq"], lw["gk"], cos, sin)


def _pool_kernel(prev_ref, cur_ref, next_ref, w_ref, sc_ref, o_ref, *, seq, tp):
    i = pl.program_id(1)
    nt = pl.num_programs(1)
    cur = cur_ref[0]
    prev = jnp.where(i > 0, prev_ref[0], 0.0)
    nxt = jnp.where(i < nt - 1, next_ref[0], 0.0)
    ext = jnp.concatenate([prev, cur, nxt], axis=0)
    n = tp + 2 * HALO
    t = i * tp + lax.broadcasted_iota(I32, (tp, 1), 0)

    def shifted(a, d):
        return pltpu.roll(a, (-d) % n, 0)

    outs = []
    for gi, win in enumerate(POOL_WINDOWS):
        xg = ext[:, gi * POOL_CH:(gi + 1) * POOL_CH]
        s = xg + shifted(xg, -1)
        half = 1
        while 2 * half < win:
            s = shifted(s, -half) + shifted(s, half)
            half *= 2
        left = win // 2
        right = win - 1 - left
        cnt = (jnp.minimum(t + right + 1, seq) - jnp.maximum(t - left, 0)).astype(F32)
        dlt = s[HALO:HALO + tp] / cnt - cur[:, gi * POOL_CH:(gi + 1) * POOL_CH]
        outs.append(_dot(dlt.astype(BF16), w_ref[gi]))
    o_ref[0] = (jnp.concatenate(outs, axis=1) * sc_ref[...]).astype(BF16)


def _pool(u, lw):
    b, seq, _ = u.shape
    tp = min(512, seq)
    r = tp // HALO
    nh = seq // HALO
    return pl.pallas_call(
        functools.partial(_pool_kernel, seq=seq, tp=tp),
        grid=(b, seq // tp),
        in_specs=[pl.BlockSpec((1, HALO, D_POOL), lambda bi, i: (bi, jnp.maximum(i * r - 1, 0), 0)),
                  pl.BlockSpec((1, tp, D_POOL), lambda bi, i: (bi, i, 0)),
                  pl.BlockSpec((1, HALO, D_POOL), lambda bi, i: (bi, jnp.minimum((i + 1) * r, nh - 1), 0)),
                  _const_spec((4, POOL_CH, POOL_CH)), _const_spec((1, D_POOL))],
        out_specs=pl.BlockSpec((1, tp, D_POOL), lambda bi, i: (bi, i, 0)),
        out_shape=jax.ShapeDtypeStruct((b, seq, D_POOL), BF16),
        compiler_params=_cparams(("arbitrary", "arbitrary")),
        name="pool",
    )(u, u, u, lw["pool_w"], lw["pool_scale"])


def _attn_kernel(q_ref, k_ref, v_ref, lam_ref, g_ref, o_ref, q_s, m_s, l_s, acc_s, *, post_scale):
    ki = pl.program_id(3)

    @pl.when(ki == 0)
    def _():
        q = q_ref[0]
        lane = lax.broadcasted_iota(I32, q.shape, 1)
        zero = jnp.zeros_like(q)
        q_s[0] = jnp.where(lane < DA_DH, q, zero)
        q_s[1] = jnp.where(lane >= DA_DH, q, zero)
        m_s[...] = jnp.full(m_s.shape, -jnp.inf, F32)
        l_s[...] = jnp.zeros(l_s.shape, F32)
        acc_s[...] = jnp.zeros(acc_s.shape, F32)

    k = k_ref[0]
    v = v_ref[0]
    for c in range(2):
        s = _dot_nt(q_s[c], k)
        m_prev = m_s[c]
        m_new = jnp.maximum(m_prev, jnp.max(s, axis=1, keepdims=True))
        alpha = jnp.exp(m_prev - m_new)
        p = jnp.exp(s - m_new)
        l_s[c] = alpha * l_s[c] + jnp.sum(p, axis=1, keepdims=True)
        acc_s[c] = alpha * acc_s[c] + _dot(p.astype(BF16), v)
        m_s[c] = m_new

    @pl.when(ki == pl.num_programs(3) - 1)
    def _():
        o = acc_s[0] / l_s[0] - lam_ref[...] * (acc_s[1] / l_s[1])
        o_ref[0] = (_rms(o, g_ref[...]) * post_scale).astype(BF16)


def _attention(q, k, v, lam, subln_g, lam_init):
    b, seq, _ = q.shape
    tq = min(512, seq)
    tk = min(512, seq)
    qspec = pl.BlockSpec((1, tq, DA_DV), lambda bi, h, qi, ki: (bi, qi, h))
    kspec = pl.BlockSpec((1, tk, DA_DV), lambda bi, h, qi, ki: (bi, ki, h))
    return pl.pallas_call(
        functools.partial(_attn_kernel, post_scale=1.0 - lam_init),
        grid=(b, DA_HEADS, seq // tq, seq // tk),
        in_specs=[qspec, kspec, kspec, _const_spec((1, DA_DV)), _const_spec((1, DA_DV))],
        out_specs=qspec,
        out_shape=jax.ShapeDtypeStruct((b, seq, D_DA), BF16),
        scratch_shapes=[pltpu.VMEM((2, tq, DA_DV), BF16), pltpu.VMEM((2, tq, 1), F32),
                        pltpu.VMEM((2, tq, 1), F32), pltpu.VMEM((2, tq, DA_DV), F32)],
        compiler_params=_cparams(("arbitrary", "arbitrary", "arbitrary", "arbitrary")),
        name="diff_attn",
    )(q, k, v, lam, subln_g)


def _ssd_kernel(*refs, rev):
    if rev:
        (prev_ref, cur_ref, next_ref, dt_ref, cw_ref, cb_ref, dtb_ref, alog_ref,
         yf_ref, z_ref, dsk_ref, gn_ref, o_ref, ext_s, st_s) = refs
    else:
        (prev_ref, cur_ref, next_ref, dt_ref, cw_ref, cb_ref, dtb_ref, alog_ref, o_ref, ext_s, st_s) = refs
    c = pl.program_id(1)
    nc = pl.num_programs(1)
    cc = nc - 1 - c if rev else c
    d = 1 if rev else 0

    @pl.when(c == 0)
    def _():
        st_s[...] = jnp.zeros(st_s.shape, F32)

    ext_s[0:HALO] = jnp.where(cc > 0, prev_ref[0].astype(F32), 0.0)
    ext_s[HALO:HALO + CHUNK] = cur_ref[0].astype(F32)
    ext_s[HALO + CHUNK:] = jnp.where(cc < nc - 1, next_ref[0].astype(F32), 0.0)
    acc = jnp.broadcast_to(cb_ref[...], (CHUNK, D_XBC))
    for kk in range(CONV_K):
        acc = acc + ext_s[pl.ds(HALO - CONV_K // 2 + kk, CHUNK), :] * cw_ref[kk:kk + 1, :]
    xc = _silu(acc)
    xs = xc[:, :D_SSD]
    gn = SSD_GROUPS * SSD_STATE
    bm = xc[:, D_SSD:D_SSD + gn]
    cm = xc[:, D_SSD + gn:]

    x = dt_ref[0] + dtb_ref[...]
    dtv = jnp.maximum(x, 0.0) + jnp.log1p(jnp.exp(-jnp.abs(x)))
    a = dtv * (-jnp.exp(alog_ref[...]))
    ri = lax.broadcasted_iota(I32, (CHUNK, CHUNK), 0)
    ci = lax.broadcasted_iota(I32, (CHUNK, CHUNK), 1)
    mask = (ci >= ri) if rev else (ci <= ri)
    tri = jnp.where(mask, 1.0, 0.0).astype(BF16)
    a1, a2, a3 = _split3(a)
    cs = _dot(tri, a1) + _dot(tri, a2) + _dot(tri, a3)
    cs_t = cs.T
    last = 0 if rev else CHUNK - 1
    cs_end = cs[last:last + 1, :]
    e_dec = jnp.exp(cs)
    w_dec = dtv * jnp.exp(cs_end - cs)

    er = lax.broadcasted_iota(I32, (LANES, D_SSD), 0)
    ec = lax.broadcasted_iota(I32, (LANES, D_SSD), 1)
    expand = jnp.where(er == d * SSD_HEADS + ec // SSD_HEAD_DIM, 1.0, 0.0).astype(BF16)

    def widen(vv):
        hi, lo = _split2(vv)
        return _dot(hi, expand) + _dot(lo, expand)

    xdt = (xs * widen(dtv)).astype(BF16)
    xw = (xs * widen(w_dec)).astype(BF16)
    e_wide = widen(e_dec)

    hg = SSD_HEADS // SSD_GROUPS
    gw = hg * SSD_HEAD_DIM
    lane = lax.broadcasted_iota(I32, (CHUNK, LANES), 1)
    ys = []
    for g in range(SSD_GROUPS):
        bg = bm[:, g * SSD_STATE:(g + 1) * SSD_STATE]
        cg = cm[:, g * SSD_STATE:(g + 1) * SSD_STATE].astype(BF16)
        cb = _dot_nt(cg, bg.astype(BF16))
        st = st_s[g]
        y_off = _dot(cg, st.astype(BF16)) * e_wide[:, g * gw:(g + 1) * gw]
        st_s[g] = st * e_wide[last:last + 1, g * gw:(g + 1) * gw] + _dot(bg.T.astype(BF16), xw[:, g * gw:(g + 1) * gw])
        for pr in range(hg // 2):
            ms_ = []
            for hh in range(2):
                col = d * SSD_HEADS + g * hg + pr * 2 + hh
                diff = cs[:, col:col + 1] - cs_t[col:col + 1, :]
                ms_.append((cb * jnp.exp(jnp.where(mask, diff, -1e30))).astype(BF16))
            q0 = g * hg // 2 + pr
            xp = xdt[:, q0 * LANES:(q0 + 1) * LANES]
            zero = jnp.zeros_like(xp)
            rhs = jnp.concatenate([jnp.where(lane < SSD_HEAD_DIM, xp, zero),
                                   jnp.where(lane >= SSD_HEAD_DIM, xp, zero)], axis=0)
            yd = _dot(jnp.concatenate(ms_, axis=1), rhs)
            ys.append(yd + y_off[:, pr * LANES:(pr + 1) * LANES])
    y = jnp.concatenate(ys, axis=1)

    if rev:
        y = yf_ref[0] + y + xs * dsk_ref[...]
        y = y * _silu(z_ref[0].astype(F32))
        o_ref[0] = _rms(y, gn_ref[...]).astype(BF16)
    else:
        o_ref[0] = y


def _ssd(xbc, dt, z, lw):
    b, seq, _ = xbc.shape
    nc = seq // CHUNK
    r = CHUNK // HALO
    nh = seq // HALO

    def specs(rev):
        ch = (lambda c: nc - 1 - c) if rev else (lambda c: c)
        return [pl.BlockSpec((1, HALO, D_XBC), lambda bi, c: (bi, jnp.maximum(ch(c) * r - 1, 0), 0)),
                pl.BlockSpec((1, CHUNK, D_XBC), lambda bi, c: (bi, ch(c), 0)),
                pl.BlockSpec((1, HALO, D_XBC), lambda bi, c: (bi, jnp.minimum((ch(c) + 1) * r, nh - 1), 0)),
                pl.BlockSpec((1, CHUNK, LANES), lambda bi, c: (bi, ch(c), 0)),
                _const_spec((8, D_XBC)), _const_spec((1, D_XBC)), _const_spec((1, LANES)), _const_spec((1, LANES))]

    scratch = [pltpu.VMEM((CHUNK + 2 * HALO, D_XBC), F32),
               pltpu.VMEM((SSD_GROUPS, SSD_STATE, D_SSD // SSD_GROUPS), F32)]
    common = (xbc, xbc, xbc, dt, lw["conv_w"], lw["conv_b"], lw["dt_bias"], lw["a_log"])
    y_f = pl.pallas_call(
        functools.partial(_ssd_kernel, rev=False),
        grid=(b, nc),
        in_specs=specs(False),
        out_specs=pl.BlockSpec((1, CHUNK, D_SSD), lambda bi, c: (bi, c, 0)),
        out_shape=jax.ShapeDtypeStruct((b, seq, D_SSD), F32),
        scratch_shapes=scratch,
        compiler_params=_cparams(("arbitrary", "arbitrary")),
        name="ssd_fwd",
    )(*common)
    rspec = pl.BlockSpec((1, CHUNK, D_SSD), lambda bi, c: (bi, nc - 1 - c, 0))
    return pl.pallas_call(
        functools.partial(_ssd_kernel, rev=True),
        grid=(b, nc),
        in_specs=specs(True) + [rspec, rspec, _const_spec((1, D_SSD)), _const_spec((1, D_SSD))],
        out_specs=rspec,
        out_shape=jax.ShapeDtypeStruct((b, seq, D_SSD), BF16),
        scratch_shapes=scratch,
        compiler_params=_cparams(("arbitrary", "arbitrary")),
        name="ssd_bwd",
    )(*common, y_f, z, lw["d_skip"], lw["ssd_norm_g"])


def _first_argmax(v, rowid, big):
    vmax = jnp.max(v, axis=0, keepdims=True)
    idx = jnp.min(jnp.where(v == vmax, rowid, big), axis=0, keepdims=True)
    return vmax, idx


def _outproj_kernel(x_ref, yp_ref, ya_ref, ys_ref, wp_ref, wa_ref, ws_ref, g_ref, wrh_ref, wrl_ref, rb_ref,
                    xo_ref, hf_ref, rt_ref, rtt_ref):
    x = x_ref[...] + (_dot(yp_ref[...], wp_ref[...]) + _dot(ya_ref[...], wa_ref[...]) + _dot(ys_ref[...], ws_ref[...]))
    xo_ref[...] = x
    hi, lo = _split2(_rms(x, g_ref[...]))
    hb = pltpu.bitcast(hi.astype(F32), U32)
    half = D_MODEL // 2
    hf_ref[...] = hb[:, :half] | (hb[:, half:] >> 16)
    wrh = wrh_ref[...]
    lg = _dot(hi, wrh) + _dot(lo, wrh) + _dot(hi, wrl_ref[...]) + rb_ref[...]
    lgt = lg.T
    tm = lgt.shape[1]
    rowid = lax.broadcasted_iota(I32, (8, tm), 0)
    coarse = jnp.where(rowid < N_GROUPS, lgt[0:8], -jnp.inf)
    cmax = jnp.max(coarse, axis=0, keepdims=True)
    ce = jnp.exp(coarse - cmax)
    _, grp = _first_argmax(ce, rowid, 99)
    gp = 1.0 / jnp.sum(ce, axis=0, keepdims=True)
    fine = jnp.zeros((8, tm), F32)
    for g in range(N_GROUPS):
        fine = jnp.where(grp == g, lgt[8 + 8 * g:16 + 8 * g], fine)
    fe = jnp.exp(fine - jnp.max(fine, axis=0, keepdims=True))
    fp = fe / jnp.sum(fe, axis=0, keepdims=True)
    v0, i0 = _first_argmax(fp, rowid, 99)
    v1, i1 = _first_argmax(jnp.where(rowid == i0, -1.0, fp), rowid, 99)
    den = v0 + v1
    vals = [(grp * EXP_PER_GROUP + i0).astype(F32), (grp * EXP_PER_GROUP + i1).astype(F32), gp * v0 / den, gp * v1 / den]
    rtt = jnp.zeros((8, tm), F32)
    for j, vv in enumerate(vals):
        rtt = jnp.where(rowid == j, vv, rtt)
    rtt_ref[...] = rtt
    rt_ref[...] = jnp.concatenate([rtt, jnp.zeros((LANES - 8, tm), F32)], axis=0).T


def _outproj(x, y_pool, y_da, y_ssd, lw):
    n = x.shape[0]
    tm = min(512, n)
    row = lambda w: pl.BlockSpec((tm, w), lambda i: (i, 0))
    ws = [lw["wo_pool"], lw["wo_da"], lw["wo_ssd"]]
    return pl.pallas_call(
        _outproj_kernel,
        grid=(n // tm,),
        in_specs=[row(D_MODEL), row(D_POOL), row(D_DA), row(D_SSD)] + [_const_spec(w.shape) for w in ws]
        + [_const_spec((1, D_MODEL)), _const_spec((D_MODEL, LANES)), _const_spec((D_MODEL, LANES)), _const_spec((1, LANES))],
        out_specs=[row(D_MODEL), row(D_MODEL // 2), row(LANES), pl.BlockSpec((8, tm), lambda i: (0, i))],
        out_shape=[jax.ShapeDtypeStruct((n, D_MODEL), F32), jax.ShapeDtypeStruct((n, D_MODEL // 2), U32),
                   jax.ShapeDtypeStruct((n, LANES), F32), jax.ShapeDtypeStruct((8, n), F32)],
        compiler_params=_cparams(("arbitrary",)),
        name="outproj_router",
    )(x, y_pool, y_da, y_ssd, *ws, lw["g_ffn"], lw["wr_hi"], lw["wr_lo"], lw["r_bias"])


def _rank_kernel(rtt_ref, rank_ref, cnt_ref):
    @pl.when(pl.program_id(0) == 0)
    def _():
        cnt_ref[...] = jnp.zeros(cnt_ref.shape, F32)

    t = rtt_ref.shape[1]
    eid = lax.broadcasted_iota(I32, (N_EXPERTS, t), 0)
    oh0 = jnp.where(eid == rtt_ref[0:1, :].astype(I32), 1.0, 0.0)
    oh1 = jnp.where(eid == rtt_ref[1:2, :].astype(I32), 1.0, 0.0)
    oh = oh0 + oh1
    before = jnp.where(lax.broadcasted_iota(I32, (t, t), 0) < lax.broadcasted_iota(I32, (t, t), 1), 1.0, 0.0)
    pre = _dot(oh.astype(BF16), before.astype(BF16)) + cnt_ref[:, 0:1]
    r0 = jnp.sum(oh0 * pre, axis=0, keepdims=True)
    r1 = jnp.sum(oh1 * pre, axis=0, keepdims=True)
    rowid = lax.broadcasted_iota(I32, (8, t), 0)
    rank_ref[...] = jnp.where(rowid == 0, r0, jnp.where(rowid == 1, r1, 0.0))
    cnt_ref[...] = cnt_ref[...] + jnp.sum(oh, axis=1, keepdims=True)


def _rank(rtt):
    n = rtt.shape[1]
    t = min(512, n)
    return pl.pallas_call(
        _rank_kernel,
        grid=(n // t,),
        in_specs=[pl.BlockSpec((8, t), lambda i: (0, i))],
        out_specs=[pl.BlockSpec((8, t), lambda i: (0, i)), pl.BlockSpec((N_EXPERTS, LANES), lambda i: (0, 0))],
        out_shape=[jax.ShapeDtypeStruct((8, n), F32), jax.ShapeDtypeStruct((N_EXPERTS, LANES), F32)],
        compiler_params=_cparams(("arbitrary",)),
        name="moe_rank",
    )(rtt)


def _row_copy(src, i, dst, j, sem):
    return pltpu.make_async_copy(src.at[pl.ds(i, 1)], dst.at[pl.ds(j, 1)], sem)


def _dispatch_kernel(dest_ref, hf_ref, xs_in_ref, xs_ref, sem):
    del xs_in_ref
    tm = hf_ref.shape[0]

    def start(j, carry):
        _row_copy(hf_ref, j, xs_ref, dest_ref[0, 0, j], sem).start()
        _row_copy(hf_ref, j, xs_ref, dest_ref[0, 0, tm + j], sem).start()
        return carry

    lax.fori_loop(0, tm, start, 0)

    def wait(j, carry):
        _row_copy(hf_ref, 0, xs_ref, 0, sem).wait()
        _row_copy(hf_ref, 0, xs_ref, 0, sem).wait()
        return carry

    lax.fori_loop(0, tm, wait, 0)


def _dispatch(dest, hfp, n_slots, tm):
    n = hfp.shape[0]
    half = D_MODEL // 2
    return pl.pallas_call(
        _dispatch_kernel,
        grid=(n // tm,),
        in_specs=[pl.BlockSpec((1, 1, 2 * tm), lambda i: (i, 0, 0), memory_space=pltpu.SMEM),
                  pl.BlockSpec((tm, half), lambda i: (i, 0)),
                  pl.BlockSpec(memory_space=pl.ANY)],
        out_specs=pl.BlockSpec(memory_space=pl.ANY),
        out_shape=jax.ShapeDtypeStruct((n_slots, half), U32),
        scratch_shapes=[pltpu.SemaphoreType.DMA(())],
        input_output_aliases={2: 0},
        compiler_params=_cparams(("arbitrary",)),
        name="moe_dispatch",
    )(dest, hfp, jnp.zeros((n_slots, half), U32))


def _expert_kernel(be_ref, nu_ref, xs_ref, wgu_ref, wdn_ref, o_ref):
    del be_ref
    i = pl.program_id(0)

    @pl.when(i < nu_ref[0])
    def _():
        p = xs_ref[...]
        xa = pltpu.bitcast(p & jnp.uint32(0xFFFF0000), F32).astype(BF16)
        xb = pltpu.bitcast(p << 16, F32).astype(BF16)
        gu = _dot(jnp.concatenate([xa, xb], axis=1), wgu_ref[0])
        act = (_silu(gu[:, :D_EXPERT]) * gu[:, D_EXPERT:]).astype(BF16)
        o_ref[...] = _dot(act, wdn_ref[0])

    @pl.when(i >= nu_ref[0])
    def _():
        o_ref[...] = jnp.zeros(o_ref.shape, F32)


def _experts(blk_exp, n_used, xs, lw, tb):
    n_slots = xs.shape[0]
    half = D_MODEL // 2
    grid_spec = pltpu.PrefetchScalarGridSpec(
        num_scalar_prefetch=2,
        grid=(n_slots // tb,),
        in_specs=[pl.BlockSpec((tb, half), lambda i, be, nu: (i, 0)),
                  pl.BlockSpec((1, D_MODEL, 2 * D_EXPERT), lambda i, be, nu: (be[i], 0, 0)),
                  pl.BlockSpec((1, D_EXPERT, D_MODEL), lambda i, be, nu: (be[i], 0, 0))],
        out_specs=pl.BlockSpec((tb, D_MODEL), lambda i, be, nu: (i, 0)),
    )
    return pl.pallas_call(
        _expert_kernel,
        grid_spec=grid_spec,
        out_shape=jax.ShapeDtypeStruct((n_slots, D_MODEL), F32),
        compiler_params=_cparams(("arbitrary",)),
        name="moe_experts",
    )(blk_exp, n_used, xs, lw["w_gu"], lw["w_dn"])


def _combine_kernel(dest_ref, x_ref, rt_ref, pe_ref, g_ref, wpg_ref, bpg_ref, wple_ref, outs_ref, o_ref, g0_s, g1_s, sem):
    tm = x_ref.shape[0]

    def start(j, carry):
        _row_copy(outs_ref, dest_ref[0, 0, j], g0_s, j, sem).start()
        _row_copy(outs_ref, dest_ref[0, 0, tm + j], g1_s, j, sem).start()
        return carry

    lax.fori_loop(0, tm, start, 0)

    def wait(j, carry):
        _row_copy(outs_ref, 0, g0_s, 0, sem).wait()
        _row_copy(outs_ref, 0, g1_s, 0, sem).wait()
        return carry

    lax.fori_loop(0, tm, wait, 0)

    rt = rt_ref[...]
    x = x_ref[...] + (g0_s[...] * rt[:, 2:3] + g1_s[...] * rt[:, 3:4])
    hn = _rms(x, g_ref[...]).astype(BF16)
    gate = jax.nn.sigmoid(_dot(hn, wpg_ref[...]) + bpg_ref[...])
    o_ref[...] = x + gate * _dot(pe_ref[...].astype(BF16), wple_ref[...])


def _combine(dest, x, rt, pe, outs, lw, tm):
    n = x.shape[0]
    row = lambda w: pl.BlockSpec((tm, w), lambda i: (i, 0))
    return pl.pallas_call(
        _combine_kernel,
        grid=(n // tm,),
        in_specs=[pl.BlockSpec((1, 1, 2 * tm), lambda i: (i, 0, 0), memory_space=pltpu.SMEM),
                  row(D_MODEL), row(LANES), row(D_PLE), _const_spec((1, D_MODEL)),
                  _const_spec((D_MODEL, D_MODEL)), _const_spec((1, D_MODEL)), _const_spec((D_PLE, D_MODEL)),
                  pl.BlockSpec(memory_space=pl.ANY)],
        out_specs=row(D_MODEL),
        out_shape=jax.ShapeDtypeStruct((n, D_MODEL), F32),
        scratch_shapes=[pltpu.VMEM((tm, D_MODEL), F32), pltpu.VMEM((tm, D_MODEL), F32), pltpu.SemaphoreType.DMA(())],
        compiler_params=_cparams(("arbitrary",)),
        name="moe_combine_ple",
    )(dest, x, rt, pe, lw["g_ple"], lw["w_pg"], lw["b_pg"], lw["w_ple"], outs)


def _tile_dest(dest, tm):
    n = dest.shape[1]
    return dest.reshape(2, n // tm, tm).transpose(1, 0, 2).reshape(n // tm, 1, 2 * tm)


def _moe_ple(x, hfp, rt, rtt, pe, lw):
    n = x.shape[0]
    tb = min(256, n)
    rank, cnt = _rank(rtt)
    counts = cnt[:, 0].astype(I32)
    padded = (counts + tb - 1) // tb * tb
    pend = jnp.cumsum(padded)
    pstart = pend - padded
    e = rtt[0:2].astype(I32)
    dest = pstart[e] + rank[0:2].astype(I32)
    n_blocks = -(-2 * n // tb) + N_EXPERTS
    blk_exp = jnp.minimum(jnp.searchsorted(pend, jnp.arange(n_blocks, dtype=I32) * tb, side="right"),
                          N_EXPERTS - 1).astype(I32)
    n_used = (pend[-1:] // tb).astype(I32)
    td = min(512, n)
    xs = _dispatch(_tile_dest(dest, td), hfp, n_blocks * tb, td)
    outs = _experts(blk_exp, n_used, xs, lw, tb)
    tc = min(256, n)
    return _combine(_tile_dest(dest, tc), x, rt, pe, outs, lw, tc)


def _layer(x, pe, lw, cos, sin, lam_init):
    b, seq, _ = x.shape
    n = b * seq
    pool_u, q, k, v, z, xbc, dt = _inproj(x.reshape(n, D_MODEL), lw, cos, sin, seq)
    sh = lambda a: a.reshape(b, seq, a.shape[-1])
    y_pool = _pool(sh(pool_u), lw)
    y_da = _attention(sh(q), sh(k), sh(v), lw["lam"], lw["subln_g"], lam_init)
    y_ssd = _ssd(sh(xbc), sh(dt), sh(z), lw)
    x2, hfp, rt, rtt = _outproj(x.reshape(n, D_MODEL), y_pool.reshape(n, D_POOL), y_da.reshape(n, D_DA),
                                y_ssd.reshape(n, D_SSD), lw)
    x3 = _moe_ple(x2, hfp, rt, rtt, pe.reshape(n, D_PLE), lw)
    return x3.reshape(b, seq, D_MODEL)


def _rope_tables(seq):
    half = DA_DH // 2
    inv = jnp.power(ROPE_THETA, -jnp.arange(half, dtype=F32) * 2.0 / DA_DH)
    ang = jnp.arange(seq, dtype=F32)[:, None] * inv[None, :]
    cos = jnp.cos(ang)
    sin = jnp.sin(ang)
    return jnp.tile(cos, (1, 4)), jnp.tile(jnp.concatenate([-sin, sin], axis=1), (1, 2))


def _prep_layer(i, w):
    row = lambda a: a.reshape(1, -1).astype(F32)
    w_in = w["w_in"][i].astype(BF16)
    o = 0
    lw = {"g_mix": row(w["norm_mix_g"][i])}
    for name, width in (("w_pool", D_POOL), ("w_q", D_DA), ("w_k", D_DA), ("w_v", D_DA), ("w_z", D_SSD), ("w_xbc", D_XBC)):
        lw[name] = w_in[:, o:o + width]
        o += width
    lw["w_dt"] = jnp.pad(w_in[:, o:], ((0, 0), (0, LANES - 2 * SSD_HEADS)))
    lw["gq"] = row(jnp.tile(w["q_norm_g"][i], 2 * DA_HEADS)) * (DA_DH ** -0.5)
    lw["gk"] = row(jnp.tile(w["k_norm_g"][i], 2 * DA_HEADS))
    lw["pool_w"] = w["pool_w"][i].astype(BF16)
    lw["pool_scale"] = row(w["pool_scale"][i])
    lf = w["lam_qk"][i].astype(F32)
    lam_init = 0.8 - 0.6 * math.exp(-0.3 * i)
    lam = jnp.exp(jnp.sum(lf[0] * lf[1])) - jnp.exp(jnp.sum(lf[2] * lf[3])) + lam_init
    lw["lam"] = jnp.full((1, DA_DV), lam, F32)
    lw["subln_g"] = row(w["subln_g"][i])
    lw["conv_w"] = jnp.pad(w["conv_w"][i].astype(F32), ((0, 8 - CONV_K), (0, 0)))
    lw["conv_b"] = row(w["conv_b"][i])
    pad_row = lambda a: jnp.pad(a.reshape(1, -1).astype(F32), ((0, 0), (0, LANES - 2 * SSD_HEADS)))
    lw["dt_bias"] = pad_row(w["dt_bias"][i])
    lw["a_log"] = pad_row(w["a_log"][i])
    lw["d_skip"] = row(jnp.repeat(w["d_skip"][i], SSD_HEAD_DIM))
    lw["ssd_norm_g"] = row(w["ssd_norm_g"][i])
    w_out = w["w_out"][i].astype(BF16)
    lw["wo_pool"] = w_out[:D_POOL]
    lw["wo_da"] = w_out[D_POOL:D_POOL + D_DA]
    lw["wo_ssd"] = w_out[D_POOL + D_DA:]
    lw["g_ffn"] = row(w["norm_ffn_g"][i])
    wr = jnp.zeros((D_MODEL, LANES), F32)
    wr = wr.at[:, :N_GROUPS].set(w["router_coarse_w"][i]).at[:, 8:8 + N_EXPERTS].set(w["router_fine_w"][i])
    lw["wr_hi"] = wr.astype(BF16)
    lw["wr_lo"] = (wr - lw["wr_hi"].astype(F32)).astype(BF16)
    rb = jnp.zeros((1, LANES), F32)
    lw["r_bias"] = rb.at[0, :N_GROUPS].set(w["router_coarse_b"][i]).at[0, 8:8 + N_EXPERTS].set(w["router_fine_b"][i])
    lw["w_gu"] = w["w_gate_up"][i].astype(BF16)
    lw["w_dn"] = w["w_down"][i].astype(BF16)
    lw["g_ple"] = row(w["norm_ple_g"][i])
    lw["w_pg"] = w["w_ple_gate"][i].astype(BF16)
    lw["b_pg"] = row(w["b_ple_gate"][i])
    lw["w_ple"] = w["w_ple"][i].astype(BF16)
    return lw, lam_init


def kernel(x_prompt, x_sample, p_prompt, p_sample, w_in, w_out, pool_w, pool_scale, q_norm_g, k_norm_g, lam_qk, subln_g, conv_w, conv_b, a_log, dt_bias, d_skip, ssd_norm_g, norm_mix_g, norm_ffn_g, router_coarse_w, router_coarse_b, router_fine_w, router_fine_b, w_gate_up, w_down, norm_ple_g, w_ple, w_ple_gate, b_ple_gate):
    w = dict(w_in=w_in, w_out=w_out, pool_w=pool_w, pool_scale=pool_scale, q_norm_g=q_norm_g, k_norm_g=k_norm_g,
             lam_qk=lam_qk, subln_g=subln_g, conv_w=conv_w, conv_b=conv_b, a_log=a_log, dt_bias=dt_bias,
             d_skip=d_skip, ssd_norm_g=ssd_norm_g, norm_mix_g=norm_mix_g, norm_ffn_g=norm_ffn_g,
             router_coarse_w=router_coarse_w, router_coarse_b=router_coarse_b, router_fine_w=router_fine_w,
             router_fine_b=router_fine_b, w_gate_up=w_gate_up, w_down=w_down, norm_ple_g=norm_ple_g,
             w_ple=w_ple, w_ple_gate=w_ple_gate, b_ple_gate=b_ple_gate)
    depth = w_in.shape[0]
    xs = [x_prompt, x_sample]
    ps = [p_prompt, p_sample]
    tables = [_rope_tables(x.shape[1]) for x in xs]
    for i in range(depth):
        lw, lam_init = _prep_layer(i, w)
        xs = [_layer(x, p[i], lw, cs[0], cs[1], lam_init) for x, p, cs in zip(xs, ps, tables)]
    return tuple(xs)
```

```python
import functools
import math

import jax
import jax.numpy as jnp
from jax import lax
from jax.experimental import pallas as pl
from jax.experimental.pallas import tpu as pltpu

F32 = jnp.float32
BF16 = jnp.bfloat16
I32 = jnp.int32
U32 = jnp.uint32

EPS = 1e-6
D_MODEL = 2048
D_PLE = 256
POOL_WINDOWS = (2, 4, 8, 16)
POOL_CH = 128
D_POOL = 512
DA_HEADS = 4
DA_DH = 64
DA_DV = 128
D_DA = 512
ROPE_THETA = 10000.0
SSD_HEADS = 16
SSD_HEAD_DIM = 64
D_SSD = 1024
SSD_GROUPS = 2
SSD_STATE = 128
CONV_K = 5
CHUNK = 128
D_XBC = 1536
N_GROUPS = 4
EXP_PER_GROUP = 8
N_EXPERTS = 32
D_EXPERT = 512

LANES = 128
HALO = 16
VMEM_LIMIT = 56 * 2**20


def _cparams(sem):
    return pltpu.CompilerParams(dimension_semantics=sem, vmem_limit_bytes=VMEM_LIMIT)


def _const_spec(shape):
    nd = len(shape)
    return pl.BlockSpec(shape, lambda *_: (0,) * nd, pipeline_mode=pl.Buffered(1))


def _dot(a, b):
    return jnp.dot(a, b, preferred_element_type=F32)


def _dot_nt(a, b):
    return lax.dot_general(a, b, (((1,), (1,)), ((), ())), preferred_element_type=F32)


def _split2(x):
    hi = x.astype(BF16)
    lo = (x - hi.astype(F32)).astype(BF16)
    return hi, lo


def _split3(x):
    a1 = x.astype(BF16)
    r1 = x - a1.astype(F32)
    a2 = r1.astype(BF16)
    a3 = (r1 - a2.astype(F32)).astype(BF16)
    return a1, a2, a3


def _silu(x):
    return x * jax.nn.sigmoid(x)


def _rms(x, g):
    ms = jnp.mean(x * x, axis=-1, keepdims=True)
    return x * lax.rsqrt(ms + EPS) * g


def _qk_prep(x, g, cos, sin, bd, first_half):
    outs = []
    for c in range(D_DA // LANES):
        xc = x[:, c * LANES:(c + 1) * LANES]
        hi, lo = _split2(xc * xc)
        ss = _dot(hi, bd) + _dot(lo, bd)
        xn = xc * lax.rsqrt(ss * (1.0 / DA_DH) + EPS) * g[:, c * LANES:(c + 1) * LANES]
        sw = jnp.where(first_half, pltpu.roll(xn, LANES - 32, 1), pltpu.roll(xn, 32, 1))
        outs.append((xn * cos + sw * sin).astype(BF16))
    return jnp.concatenate(outs, axis=1)


def _inproj_kernel(x_ref, g_ref, wp_ref, wq_ref, wk_ref, wv_ref, wz_ref, wx_ref, wd_ref,
                   gq_ref, gk_ref, cos_ref, sin_ref,
                   pool_o, q_o, k_o, v_o, z_o, xbc_o, dt_o):
    h = _rms(x_ref[...], g_ref[...]).astype(BF16)
    pool_o[...] = _dot(h, wp_ref[...])
    v_o[...] = _dot(h, wv_ref[...]).astype(BF16)
    z_o[...] = _dot(h, wz_ref[...]).astype(BF16)
    xbc_o[...] = _dot(h, wx_ref[...]).astype(BF16)
    dt_o[...] = _dot(h, wd_ref[...])
    r = lax.broadcasted_iota(I32, (LANES, LANES), 0) // DA_DH
    c = lax.broadcasted_iota(I32, (LANES, LANES), 1) // DA_DH
    bd = jnp.where(r == c, 1.0, 0.0).astype(BF16)
    first_half = (lax.broadcasted_iota(I32, (1, LANES), 1) % DA_DH) < (DA_DH // 2)
    cos = cos_ref[...]
    sin = sin_ref[...]
    q_o[...] = _qk_prep(_dot(h, wq_ref[...]), gq_ref[...], cos, sin, bd, first_half)
    k_o[...] = _qk_prep(_dot(h, wk_ref[...]), gk_ref[...], cos, sin, bd, first_half)


def _inproj(x, lw, cos, sin, seq):
    n = x.shape[0]
    tm = min(512, seq)
    nseq = seq // tm
    row = lambda w: pl.BlockSpec((tm, w), lambda i: (i, 0))
    pos = pl.BlockSpec((tm, LANES), lambda i: (i % nseq, 0))
    ws = [lw["w_pool"], lw["w_q"], lw["w_k"], lw["w_v"], lw["w_z"], lw["w_xbc"], lw["w_dt"]]
    return pl.pallas_call(
        _inproj_kernel,
        grid=(n // tm,),
        in_specs=[row(D_MODEL), _const_spec((1, D_MODEL))] + [_const_spec(w.shape) for w in ws]
        + [_const_spec((1, D_DA)), _const_spec((1, D_DA)), pos, pos],
        out_specs=[row(D_POOL), row(D_DA), row(D_DA), row(D_DA), row(D_SSD), row(D_XBC), row(LANES)],
        out_shape=[jax.ShapeDtypeStruct((n, D_POOL), F32), jax.ShapeDtypeStruct((n, D_DA), BF16),
                   jax.ShapeDtypeStruct((n, D_DA), BF16), jax.ShapeDtypeStruct((n, D_DA), BF16),
                   jax.ShapeDtypeStruct((n, D_SSD), BF16), jax.ShapeDtypeStruct((n, D_XBC), BF16),
                   jax.ShapeDtypeStruct((n, LANES), F32)],
        compiler_params=_cparams(("arbitrary",)),
        name="inproj",
    )(x, lw["g_mix"], *ws, lw["gq"], lw["gk"], cos, sin)


def _pool_kernel(prev_ref, cur_ref, next_ref, w_ref, sc_ref, o_ref, *, seq, tp):
    i = pl.program_id(1)
    nt = pl.num_programs(1)
    cur = cur_ref[0]
    prev = jnp.where(i > 0, prev_ref[0], 0.0)
    nxt = jnp.where(i < nt - 1, next_ref[0], 0.0)
    ext = jnp.concatenate([prev, cur, nxt], axis=0)
    n = tp + 2 * HALO
    t = i * tp + lax.broadcasted_iota(I32, (tp, 1), 0)

    def shifted(a, d):
        return pltpu.roll(a, (-d) % n, 0)

    outs = []
    for gi, win in enumerate(POOL_WINDOWS):
        xg = ext[:, gi * POOL_CH:(gi + 1) * POOL_CH]
        s = xg + shifted(xg, -1)
        half = 1
        while 2 * half < win:
            s = shifted(s, -half) + shifted(s, half)
            half *= 2
        left = win // 2
        right = win - 1 - left
        cnt = (jnp.minimum(t + right + 1, seq) - jnp.maximum(t - left, 0)).astype(F32)
        dlt = s[HALO:HALO + tp] / cnt - cur[:, gi * POOL_CH:(gi + 1) * POOL_CH]
        outs.append(_dot(dlt.astype(BF16), w_ref[gi]))
    o_ref[0] = (jnp.concatenate(outs, axis=1) * sc_ref[...]).astype(BF16)


def _pool(u, lw):
    b, seq, _ = u.shape
    tp = min(512, seq)
    r = tp // HALO
    nh = seq // HALO
    return pl.pallas_call(
        functools.partial(_pool_kernel, seq=seq, tp=tp),
        grid=(b, seq // tp),
        in_specs=[pl.BlockSpec((1, HALO, D_POOL), lambda bi, i: (bi, jnp.maximum(i * r - 1, 0), 0)),
                  pl.BlockSpec((1, tp, D_POOL), lambda bi, i: (bi, i, 0)),
                  pl.BlockSpec((1, HALO, D_POOL), lambda bi, i: (bi, jnp.minimum((i + 1) * r, nh - 1), 0)),
                  _const_spec((4, POOL_CH, POOL_CH)), _const_spec((1, D_POOL))],
        out_specs=pl.BlockSpec((1, tp, D_POOL), lambda bi, i: (bi, i, 0)),
        out_shape=jax.ShapeDtypeStruct((b, seq, D_POOL), BF16),
        compiler_params=_cparams(("arbitrary", "arbitrary")),
        name="pool",
    )(u, u, u, lw["pool_w"], lw["pool_scale"])


def _attn_kernel(q_ref, k_ref, v_ref, lam_ref, g_ref, o_ref, q_s, s_s, p_s, m_s, al_s, acc_s, *, post_scale, tk):
    nk = k_ref.shape[1] // tk
    q = q_ref[0]
    lane = lax.broadcasted_iota(I32, q.shape, 1)
    zero = jnp.zeros_like(q)
    q_s[0] = jnp.where(lane < DA_DH, q, zero)
    q_s[1] = jnp.where(lane >= DA_DH, q, zero)
    m_s[...] = jnp.full(m_s.shape, -jnp.inf, F32)
    acc_s[...] = jnp.zeros(acc_s.shape, F32)
    ones_col = jnp.where(lax.broadcasted_iota(I32, (tk, LANES), 1) == 0, 1.0, 0.0).astype(BF16)

    def rows(j):
        return pl.ds(pl.multiple_of(j * tk, tk), tk)

    def scores(j, slot):
        k = k_ref[0, rows(j), :]
        for c in range(2):
            s_s[slot, c] = _dot_nt(q_s[c], k)

    def softmax(slot):
        for c in range(2):
            s = s_s[slot, c]
            m_prev = m_s[c]
            m_new = jnp.maximum(m_prev, jnp.max(s, axis=1, keepdims=True))
            p_s[slot, c] = jnp.exp2(s - m_new).astype(BF16)
            al_s[slot, c] = jnp.exp2(m_prev - m_new)
            m_s[c] = m_new

    def pv(j, slot):
        v_aug = jnp.concatenate([v_ref[0, rows(j), :], ones_col], axis=1)
        for c in range(2):
            acc_s[c] = al_s[slot, c] * acc_s[c] + _dot(p_s[slot, c], v_aug)

    scores(0, 0)
    if nk > 1:
        scores(1, 1)
    softmax(0)

    def pair(t, carry):
        for par in range(2):
            j = 2 * t + 1 + par
            slot = 1 - par
            scores(j + 1, 1 - slot)
            pv(j - 1, 1 - slot)
            softmax(slot)
        return carry

    if nk > 2:
        lax.fori_loop(0, (nk - 2) // 2, pair, 0)
    if nk > 1:
        pv(nk - 2, 0)
        softmax(1)
        pv(nk - 1, 1)
    else:
        pv(0, 0)
    a0 = acc_s[0]
    a1 = acc_s[1]
    o = a0[:, :DA_DV] / a0[:, DA_DV:DA_DV + 1] - lam_ref[...] * (a1[:, :DA_DV] / a1[:, DA_DV:DA_DV + 1])
    o_ref[0] = (_rms(o, g_ref[...]) * post_scale).astype(BF16)


def _attention(q, k, v, lam, subln_g, lam_init):
    b, seq, _ = q.shape
    tq = min(1024, seq)
    tk = min(512, seq)
    nk = seq // tk
    assert nk == 1 or nk % 2 == 0
    qspec = pl.BlockSpec((1, tq, DA_DV), lambda bi, h, qi: (bi, qi, h))
    kspec = pl.BlockSpec((1, seq, DA_DV), lambda bi, h, qi: (bi, 0, h))
    return pl.pallas_call(
        functools.partial(_attn_kernel, post_scale=1.0 - lam_init, tk=tk),
        grid=(b, DA_HEADS, seq // tq),
        in_specs=[qspec, kspec, kspec, _const_spec((1, DA_DV)), _const_spec((1, DA_DV))],
        out_specs=qspec,
        out_shape=jax.ShapeDtypeStruct((b, seq, D_DA), BF16),
        scratch_shapes=[pltpu.VMEM((2, tq, DA_DV), BF16), pltpu.VMEM((2, 2, tq, tk), F32),
                        pltpu.VMEM((2, 2, tq, tk), BF16), pltpu.VMEM((2, tq, 1), F32),
                        pltpu.VMEM((2, 2, tq, 1), F32), pltpu.VMEM((2, tq, 2 * DA_DV), F32)],
        compiler_params=_cparams(("arbitrary", "arbitrary", "arbitrary")),
        name="diff_attn",
    )(q, k, v, lam, subln_g)


def _ssd_kernel(*refs, rev):
    if rev:
        (prev_ref, cur_ref, next_ref, dt_ref, cw_ref, cb_ref, dtb_ref, alog_ref,
         yf_ref, z_ref, dsk_ref, gn_ref, o_ref, ext_s, st_s) = refs
    else:
        (prev_ref, cur_ref, next_ref, dt_ref, cw_ref, cb_ref, dtb_ref, alog_ref, o_ref, ext_s, st_s) = refs
    c = pl.program_id(1)
    nc = pl.num_programs(1)
    cc = nc - 1 - c if rev else c
    d = 1 if rev else 0

    @pl.when(c == 0)
    def _():
        st_s[...] = jnp.zeros(st_s.shape, F32)

    ext_s[0:HALO] = jnp.where(cc > 0, prev_ref[0].astype(F32), 0.0)
    ext_s[HALO:HALO + CHUNK] = cur_ref[0].astype(F32)
    ext_s[HALO + CHUNK:] = jnp.where(cc < nc - 1, next_ref[0].astype(F32), 0.0)
    acc = jnp.broadcast_to(cb_ref[...], (CHUNK, D_XBC))
    for kk in range(CONV_K):
        acc = acc + ext_s[pl.ds(HALO - CONV_K // 2 + kk, CHUNK), :] * cw_ref[kk:kk + 1, :]
    xc = _silu(acc)
    xs = xc[:, :D_SSD]
    gn = SSD_GROUPS * SSD_STATE
    bm = xc[:, D_SSD:D_SSD + gn]
    cm = xc[:, D_SSD + gn:]

    x = dt_ref[0] + dtb_ref[...]
    dtv = jnp.maximum(x, 0.0) + jnp.log1p(jnp.exp(-jnp.abs(x)))
    a = dtv * (-jnp.exp(alog_ref[...]))
    ri = lax.broadcasted_iota(I32, (CHUNK, CHUNK), 0)
    ci = lax.broadcasted_iota(I32, (CHUNK, CHUNK), 1)
    mask = (ci >= ri) if rev else (ci <= ri)
    tri = jnp.where(mask, 1.0, 0.0).astype(BF16)
    a1, a2, a3 = _split3(a)
    cs = _dot(tri, a1) + _dot(tri, a2) + _dot(tri, a3)
    cs_t = cs.T
    last = 0 if rev else CHUNK - 1
    cs_end = cs[last:last + 1, :]
    e_dec = jnp.exp(cs)
    w_dec = dtv * jnp.exp(cs_end - cs)

    er = lax.broadcasted_iota(I32, (LANES, D_SSD), 0)
    ec = lax.broadcasted_iota(I32, (LANES, D_SSD), 1)
    expand = jnp.where(er == d * SSD_HEADS + ec // SSD_HEAD_DIM, 1.0, 0.0).astype(BF16)

    def widen(vv):
        hi, lo = _split2(vv)
        return _dot(hi, expand) + _dot(lo, expand)

    xdt = (xs * widen(dtv)).astype(BF16)
    xw = (xs * widen(w_dec)).astype(BF16)
    e_wide = widen(e_dec)

    hg = SSD_HEADS // SSD_GROUPS
    gw = hg * SSD_HEAD_DIM
    lane = lax.broadcasted_iota(I32, (CHUNK, LANES), 1)
    ys = []
    for g in range(SSD_GROUPS):
        bg = bm[:, g * SSD_STATE:(g + 1) * SSD_STATE]
        cg = cm[:, g * SSD_STATE:(g + 1) * SSD_STATE].astype(BF16)
        cb = _dot_nt(cg, bg.astype(BF16))
        st = st_s[g]
        y_off = _dot(cg, st.astype(BF16)) * e_wide[:, g * gw:(g + 1) * gw]
        st_s[g] = st * e_wide[last:last + 1, g * gw:(g + 1) * gw] + _dot(bg.T.astype(BF16), xw[:, g * gw:(g + 1) * gw])
        for pr in range(hg // 2):
            ms_ = []
            for hh in range(2):
                col = d * SSD_HEADS + g * hg + pr * 2 + hh
                diff = cs[:, col:col + 1] - cs_t[col:col + 1, :]
                ms_.append((cb * jnp.exp(jnp.where(mask, diff, -1e30))).astype(BF16))
            q0 = g * hg // 2 + pr
            xp = xdt[:, q0 * LANES:(q0 + 1) * LANES]
            zero = jnp.zeros_like(xp)
            rhs = jnp.concatenate([jnp.where(lane < SSD_HEAD_DIM, xp, zero),
                                   jnp.where(lane >= SSD_HEAD_DIM, xp, zero)], axis=0)
            yd = _dot(jnp.concatenate(ms_, axis=1), rhs)
            ys.append(yd + y_off[:, pr * LANES:(pr + 1) * LANES])
    y = jnp.concatenate(ys, axis=1)

    if rev:
        y = yf_ref[0] + y + xs * dsk_ref[...]
        y = y * _silu(z_ref[0].astype(F32))
        o_ref[0] = _rms(y, gn_ref[...]).astype(BF16)
    else:
        o_ref[0] = y


def _ssd(xbc, dt, z, lw):
    b, seq, _ = xbc.shape
    nc = seq // CHUNK
    r = CHUNK // HALO
    nh = seq // HALO

    def specs(rev):
        ch = (lambda c: nc - 1 - c) if rev else (lambda c: c)
        return [pl.BlockSpec((1, HALO, D_XBC), lambda bi, c: (bi, jnp.maximum(ch(c) * r - 1, 0), 0)),
                pl.BlockSpec((1, CHUNK, D_XBC), lambda bi, c: (bi, ch(c), 0)),
                pl.BlockSpec((1, HALO, D_XBC), lambda bi, c: (bi, jnp.minimum((ch(c) + 1) * r, nh - 1), 0)),
                pl.BlockSpec((1, CHUNK, LANES), lambda bi, c: (bi, ch(c), 0)),
                _const_spec((8, D_XBC)), _const_spec((1, D_XBC)), _const_spec((1, LANES)), _const_spec((1, LANES))]

    scratch = [pltpu.VMEM((CHUNK + 2 * HALO, D_XBC), F32),
               pltpu.VMEM((SSD_GROUPS, SSD_STATE, D_SSD // SSD_GROUPS), F32)]
    common = (xbc, xbc, xbc, dt, lw["conv_w"], lw["conv_b"], lw["dt_bias"], lw["a_log"])
    y_f = pl.pallas_call(
        functools.partial(_ssd_kernel, rev=False),
        grid=(b, nc),
        in_specs=specs(False),
        out_specs=pl.BlockSpec((1, CHUNK, D_SSD), lambda bi, c: (bi, c, 0)),
        out_shape=jax.ShapeDtypeStruct((b, seq, D_SSD), F32),
        scratch_shapes=scratch,
        compiler_params=_cparams(("arbitrary", "arbitrary")),
        name="ssd_fwd",
    )(*common)
    rspec = pl.BlockSpec((1, CHUNK, D_SSD), lambda bi, c: (bi, nc - 1 - c, 0))
    return pl.pallas_call(
        functools.partial(_ssd_kernel, rev=True),
        grid=(b, nc),
        in_specs=specs(True) + [rspec, rspec, _const_spec((1, D_SSD)), _const_spec((1, D_SSD))],
        out_specs=rspec,
        out_shape=jax.ShapeDtypeStruct((b, seq, D_SSD), BF16),
        scratch_shapes=scratch,
        compiler_params=_cparams(("arbitrary", "arbitrary")),
        name="ssd_bwd",
    )(*common, y_f, z, lw["d_skip"], lw["ssd_norm_g"])


def _first_argmax(v, rowid, big):
    vmax = jnp.max(v, axis=0, keepdims=True)
    idx = jnp.min(jnp.where(v == vmax, rowid, big), axis=0, keepdims=True)
    return vmax, idx


def _outproj_kernel(x_ref, yp_ref, ya_ref, ys_ref, wp_ref, wa_ref, ws_ref, g_ref, wrh_ref, wrl_ref, rb_ref,
                    xo_ref, hf_ref, rt_ref, rtt_ref):
    x = x_ref[...] + (_dot(yp_ref[...], wp_ref[...]) + _dot(ya_ref[...], wa_ref[...]) + _dot(ys_ref[...], ws_ref[...]))
    xo_ref[...] = x
    hi, lo = _split2(_rms(x, g_ref[...]))
    hb = pltpu.bitcast(hi.astype(F32), U32)
    half = D_MODEL // 2
    hf_ref[...] = hb[:, :half] | (hb[:, half:] >> 16)
    wrh = wrh_ref[...]
    lg = _dot(hi, wrh) + _dot(lo, wrh) + _dot(hi, wrl_ref[...]) + rb_ref[...]
    lgt = lg.T
    tm = lgt.shape[1]
    rowid = lax.broadcasted_iota(I32, (8, tm), 0)
    coarse = jnp.where(rowid < N_GROUPS, lgt[0:8], -jnp.inf)
    cmax = jnp.max(coarse, axis=0, keepdims=True)
    ce = jnp.exp(coarse - cmax)
    _, grp = _first_argmax(ce, rowid, 99)
    gp = 1.0 / jnp.sum(ce, axis=0, keepdims=True)
    fine = jnp.zeros((8, tm), F32)
    for g in range(N_GROUPS):
        fine = jnp.where(grp == g, lgt[8 + 8 * g:16 + 8 * g], fine)
    fe = jnp.exp(fine - jnp.max(fine, axis=0, keepdims=True))
    fp = fe / jnp.sum(fe, axis=0, keepdims=True)
    v0, i0 = _first_argmax(fp, rowid, 99)
    v1, i1 = _first_argmax(jnp.where(rowid == i0, -1.0, fp), rowid, 99)
    den = v0 + v1
    vals = [(grp * EXP_PER_GROUP + i0).astype(F32), (grp * EXP_PER_GROUP + i1).astype(F32), gp * v0 / den, gp * v1 / den]
    rtt = jnp.zeros((8, tm), F32)
    for j, vv in enumerate(vals):
        rtt = jnp.where(rowid == j, vv, rtt)
    rtt_ref[...] = rtt
    rt_ref[...] = jnp.concatenate([rtt, jnp.zeros((LANES - 8, tm), F32)], axis=0).T


def _outproj(x, y_pool, y_da, y_ssd, lw):
    n = x.shape[0]
    tm = min(512, n)
    row = lambda w: pl.BlockSpec((tm, w), lambda i: (i, 0))
    ws = [lw["wo_pool"], lw["wo_da"], lw["wo_ssd"]]
    return pl.pallas_call(
        _outproj_kernel,
        grid=(n // tm,),
        in_specs=[row(D_MODEL), row(D_POOL), row(D_DA), row(D_SSD)] + [_const_spec(w.shape) for w in ws]
        + [_const_spec((1, D_MODEL)), _const_spec((D_MODEL, LANES)), _const_spec((D_MODEL, LANES)), _const_spec((1, LANES))],
        out_specs=[row(D_MODEL), row(D_MODEL // 2), row(LANES), pl.BlockSpec((8, tm), lambda i: (0, i))],
        out_shape=[jax.ShapeDtypeStruct((n, D_MODEL), F32), jax.ShapeDtypeStruct((n, D_MODEL // 2), U32),
                   jax.ShapeDtypeStruct((n, LANES), F32), jax.ShapeDtypeStruct((8, n), F32)],
        compiler_params=_cparams(("arbitrary",)),
        name="outproj_router",
    )(x, y_pool, y_da, y_ssd, *ws, lw["g_ffn"], lw["wr_hi"], lw["wr_lo"], lw["r_bias"])


def _rank_kernel(rtt_ref, rank_ref, cnt_ref):
    @pl.when(pl.program_id(0) == 0)
    def _():
        cnt_ref[...] = jnp.zeros(cnt_ref.shape, F32)

    t = rtt_ref.shape[1]
    eid = lax.broadcasted_iota(I32, (N_EXPERTS, t), 0)
    oh0 = jnp.where(eid == rtt_ref[0:1, :].astype(I32), 1.0, 0.0)
    oh1 = jnp.where(eid == rtt_ref[1:2, :].astype(I32), 1.0, 0.0)
    oh = oh0 + oh1
    before = jnp.where(lax.broadcasted_iota(I32, (t, t), 0) < lax.broadcasted_iota(I32, (t, t), 1), 1.0, 0.0)
    pre = _dot(oh.astype(BF16), before.astype(BF16)) + cnt_ref[:, 0:1]
    r0 = jnp.sum(oh0 * pre, axis=0, keepdims=True)
    r1 = jnp.sum(oh1 * pre, axis=0, keepdims=True)
    rowid = lax.broadcasted_iota(I32, (8, t), 0)
    rank_ref[...] = jnp.where(rowid == 0, r0, jnp.where(rowid == 1, r1, 0.0))
    cnt_ref[...] = cnt_ref[...] + jnp.sum(oh, axis=1, keepdims=True)


def _rank(rtt):
    n = rtt.shape[1]
    t = min(512, n)
    return pl.pallas_call(
        _rank_kernel,
        grid=(n // t,),
        in_specs=[pl.BlockSpec((8, t), lambda i: (0, i))],
        out_specs=[pl.BlockSpec((8, t), lambda i: (0, i)), pl.BlockSpec((N_EXPERTS, LANES), lambda i: (0, 0))],
        out_shape=[jax.ShapeDtypeStruct((8, n), F32), jax.ShapeDtypeStruct((N_EXPERTS, LANES), F32)],
        compiler_params=_cparams(("arbitrary",)),
        name="moe_rank",
    )(rtt)


def _row_copy(src, i, dst, j, sem):
    return pltpu.make_async_copy(src.at[pl.ds(i, 1)], dst.at[pl.ds(j, 1)], sem)


def _dispatch_kernel(dest_ref, hf_ref, xs_in_ref, xs_ref, sem):
    del xs_in_ref
    tm = hf_ref.shape[0]

    def start(j, carry):
        _row_copy(hf_ref, j, xs_ref, dest_ref[0, 0, j], sem).start()
        _row_copy(hf_ref, j, xs_ref, dest_ref[0, 0, tm + j], sem).start()
        return carry

    lax.fori_loop(0, tm, start, 0)

    def wait(j, carry):
        _row_copy(hf_ref, 0, xs_ref, 0, sem).wait()
        _row_copy(hf_ref, 0, xs_ref, 0, sem).wait()
        return carry

    lax.fori_loop(0, tm, wait, 0)


def _dispatch(dest, hfp, n_slots, tm):
    n = hfp.shape[0]
    half = D_MODEL // 2
    return pl.pallas_call(
        _dispatch_kernel,
        grid=(n // tm,),
        in_specs=[pl.BlockSpec((1, 1, 2 * tm), lambda i: (i, 0, 0), memory_space=pltpu.SMEM),
                  pl.BlockSpec((tm, half), lambda i: (i, 0)),
                  pl.BlockSpec(memory_space=pl.ANY)],
        out_specs=pl.BlockSpec(memory_space=pl.ANY),
        out_shape=jax.ShapeDtypeStruct((n_slots, half), U32),
        scratch_shapes=[pltpu.SemaphoreType.DMA(())],
        input_output_aliases={2: 0},
        compiler_params=_cparams(("arbitrary",)),
        name="moe_dispatch",
    )(dest, hfp, jnp.zeros((n_slots, half), U32))


def _expert_kernel(be_ref, nu_ref, xs_ref, wgu_ref, wdn_ref, o_ref):
    del be_ref
    i = pl.program_id(0)

    @pl.when(i < nu_ref[0])
    def _():
        p = xs_ref[...]
        xa = pltpu.bitcast(p & jnp.uint32(0xFFFF0000), F32).astype(BF16)
        xb = pltpu.bitcast(p << 16, F32).astype(BF16)
        gu = _dot(jnp.concatenate([xa, xb], axis=1), wgu_ref[0])
        act = (_silu(gu[:, :D_EXPERT]) * gu[:, D_EXPERT:]).astype(BF16)
        o_ref[...] = _dot(act, wdn_ref[0])

    @pl.when(i >= nu_ref[0])
    def _():
        o_ref[...] = jnp.zeros(o_ref.shape, F32)


def _experts(blk_exp, n_used, xs, lw, tb):
    n_slots = xs.shape[0]
    half = D_MODEL // 2
    grid_spec = pltpu.PrefetchScalarGridSpec(
        num_scalar_prefetch=2,
        grid=(n_slots // tb,),
        in_specs=[pl.BlockSpec((tb, half), lambda i, be, nu: (i, 0)),
                  pl.BlockSpec((1, D_MODEL, 2 * D_EXPERT), lambda i, be, nu: (be[i], 0, 0)),
                  pl.BlockSpec((1, D_EXPERT, D_MODEL), lambda i, be, nu: (be[i], 0, 0))],
        out_specs=pl.BlockSpec((tb, D_MODEL), lambda i, be, nu: (i, 0)),
    )
    return pl.pallas_call(
        _expert_kernel,
        grid_spec=grid_spec,
        out_shape=jax.ShapeDtypeStruct((n_slots, D_MODEL), F32),
        compiler_params=_cparams(("arbitrary",)),
        name="moe_experts",
    )(blk_exp, n_used, xs, lw["w_gu"], lw["w_dn"])


def _combine_kernel(dest_ref, x_ref, rt_ref, pe_ref, g_ref, wpg_ref, bpg_ref, wple_ref, outs_ref, o_ref, g0_s, g1_s, sem):
    tm = x_ref.shape[0]

    def start(j, carry):
        _row_copy(outs_ref, dest_ref[0, 0, j], g0_s, j, sem).start()
        _row_copy(outs_ref, dest_ref[0, 0, tm + j], g1_s, j, sem).start()
        return carry

    lax.fori_loop(0, tm, start, 0)

    def wait(j, carry):
        _row_copy(outs_ref, 0, g0_s, 0, sem).wait()
        _row_copy(outs_ref, 0, g1_s, 0, sem).wait()
        return carry

    lax.fori_loop(0, tm, wait, 0)

    rt = rt_ref[...]
    x = x_ref[...] + (g0_s[...] * rt[:, 2:3] + g1_s[...] * rt[:, 3:4])
    hn = _rms(x, g_ref[...]).astype(BF16)
    gate = jax.nn.sigmoid(_dot(hn, wpg_ref[...]) + bpg_ref[...])
    o_ref[...] = x + gate * _dot(pe_ref[...].astype(BF16), wple_ref[...])


def _combine(dest, x, rt, pe, outs, lw, tm):
    n = x.shape[0]
    row = lambda w: pl.BlockSpec((tm, w), lambda i: (i, 0))
    return pl.pallas_call(
        _combine_kernel,
        grid=(n // tm,),
        in_specs=[pl.BlockSpec((1, 1, 2 * tm), lambda i: (i, 0, 0), memory_space=pltpu.SMEM),
                  row(D_MODEL), row(LANES), row(D_PLE), _const_spec((1, D_MODEL)),
                  _const_spec((D_MODEL, D_MODEL)), _const_spec((1, D_MODEL)), _const_spec((D_PLE, D_MODEL)),
                  pl.BlockSpec(memory_space=pl.ANY)],
        out_specs=row(D_MODEL),
        out_shape=jax.ShapeDtypeStruct((n, D_MODEL), F32),
        scratch_shapes=[pltpu.VMEM((tm, D_MODEL), F32), pltpu.VMEM((tm, D_MODEL), F32), pltpu.SemaphoreType.DMA(())],
        compiler_params=_cparams(("arbitrary",)),
        name="moe_combine_ple",
    )(dest, x, rt, pe, lw["g_ple"], lw["w_pg"], lw["b_pg"], lw["w_ple"], outs)


def _tile_dest(dest, tm):
    n = dest.shape[1]
    return dest.reshape(2, n // tm, tm).transpose(1, 0, 2).reshape(n // tm, 1, 2 * tm)


def _moe_ple(x, hfp, rt, rtt, pe, lw):
    n = x.shape[0]
    tb = min(256, n)
    rank, cnt = _rank(rtt)
    counts = cnt[:, 0].astype(I32)
    padded = (counts + tb - 1) // tb * tb
    pend = jnp.cumsum(padded)
    pstart = pend - padded
    e = rtt[0:2].astype(I32)
    eids = jnp.arange(N_EXPERTS, dtype=I32)
    start_e = jnp.sum(jnp.where(e[..., None] == eids, pstart, 0), axis=-1)
    dest = start_e + rank[0:2].astype(I32)
    n_blocks = -(-2 * n // tb) + N_EXPERTS
    blk_start = jnp.arange(n_blocks, dtype=I32) * tb
    blk_exp = jnp.minimum(jnp.sum((pend[None, :] <= blk_start[:, None]).astype(I32), axis=1), N_EXPERTS - 1)
    n_used = (pend[-1:] // tb).astype(I32)
    td = min(512, n)
    xs = _dispatch(_tile_dest(dest, td), hfp, n_blocks * tb, td)
    outs = _experts(blk_exp, n_used, xs, lw, tb)
    tc = min(256, n)
    return _combine(_tile_dest(dest, tc), x, rt, pe, outs, lw, tc)


def _layer(x, pe, lw, cos, sin, lam_init):
    b, seq, _ = x.shape
    n = b * seq
    pool_u, q, k, v, z, xbc, dt = _inproj(x.reshape(n, D_MODEL), lw, cos, sin, seq)
    sh = lambda a: a.reshape(b, seq, a.shape[-1])
    y_pool = _pool(sh(pool_u), lw)
    y_da = _attention(sh(q), sh(k), sh(v), lw["lam"], lw["subln_g"], lam_init)
    y_ssd = _ssd(sh(xbc), sh(dt), sh(z), lw)
    x2, hfp, rt, rtt = _outproj(x.reshape(n, D_MODEL), y_pool.reshape(n, D_POOL), y_da.reshape(n, D_DA),
                                y_ssd.reshape(n, D_SSD), lw)
    x3 = _moe_ple(x2, hfp, rt, rtt, pe.reshape(n, D_PLE), lw)
    return x3.reshape(b, seq, D_MODEL)


def _rope_tables(seq):
    half = DA_DH // 2
    inv = jnp.power(ROPE_THETA, -jnp.arange(half, dtype=F32) * 2.0 / DA_DH)
    ang = jnp.arange(seq, dtype=F32)[:, None] * inv[None, :]
    cos = jnp.cos(ang)
    sin = jnp.sin(ang)
    return jnp.tile(cos, (1, 4)), jnp.tile(jnp.concatenate([-sin, sin], axis=1), (1, 2))


def _prep_layer(i, w):
    row = lambda a: a.reshape(1, -1).astype(F32)
    w_in = w["w_in"][i].astype(BF16)
    o = 0
    lw = {"g_mix": row(w["norm_mix_g"][i])}
    for name, width in (("w_pool", D_POOL), ("w_q", D_DA), ("w_k", D_DA), ("w_v", D_DA), ("w_z", D_SSD), ("w_xbc", D_XBC)):
        lw[name] = w_in[:, o:o + width]
        o += width
    lw["w_dt"] = jnp.pad(w_in[:, o:], ((0, 0), (0, LANES - 2 * SSD_HEADS)))
    lw["gq"] = row(jnp.tile(w["q_norm_g"][i], 2 * DA_HEADS)) * (DA_DH ** -0.5 * math.log2(math.e))
    lw["gk"] = row(jnp.tile(w["k_norm_g"][i], 2 * DA_HEADS))
    lw["pool_w"] = w["pool_w"][i].astype(BF16)
    lw["pool_scale"] = row(w["pool_scale"][i])
    lf = w["lam_qk"][i].astype(F32)
    lam_init = 0.8 - 0.6 * math.exp(-0.3 * i)
    lam = jnp.exp(jnp.sum(lf[0] * lf[1])) - jnp.exp(jnp.sum(lf[2] * lf[3])) + lam_init
    lw["lam"] = jnp.full((1, DA_DV), lam, F32)
    lw["subln_g"] = row(w["subln_g"][i])
    lw["conv_w"] = jnp.pad(w["conv_w"][i].astype(F32), ((0, 8 - CONV_K), (0, 0)))
    lw["conv_b"] = row(w["conv_b"][i])
    pad_row = lambda a: jnp.pad(a.reshape(1, -1).astype(F32), ((0, 0), (0, LANES - 2 * SSD_HEADS)))
    lw["dt_bias"] = pad_row(w["dt_bias"][i])
    lw["a_log"] = pad_row(w["a_log"][i])
    lw["d_skip"] = row(jnp.repeat(w["d_skip"][i], SSD_HEAD_DIM))
    lw["ssd_norm_g"] = row(w["ssd_norm_g"][i])
    w_out = w["w_out"][i].astype(BF16)
    lw["wo_pool"] = w_out[:D_POOL]
    lw["wo_da"] = w_out[D_POOL:D_POOL + D_DA]
    lw["wo_ssd"] = w_out[D_POOL + D_DA:]
    lw["g_ffn"] = row(w["norm_ffn_g"][i])
    wr = jnp.zeros((D_MODEL, LANES), F32)
    wr = wr.at[:, :N_GROUPS].set(w["router_coarse_w"][i]).at[:, 8:8 + N_EXPERTS].set(w["router_fine_w"][i])
    lw["wr_hi"] = wr.astype(BF16)
    lw["wr_lo"] = (wr - lw["wr_hi"].astype(F32)).astype(BF16)
    rb = jnp.zeros((1, LANES), F32)
    lw["r_bias"] = rb.at[0, :N_GROUPS].set(w["router_coarse_b"][i]).at[0, 8:8 + N_EXPERTS].set(w["router_fine_b"][i])
    lw["w_gu"] = w["w_gate_up"][i].astype(BF16)
    lw["w_dn"] = w["w_down"][i].astype(BF16)
    lw["g_ple"] = row(w["norm_ple_g"][i])
    lw["w_pg"] = w["w_ple_gate"][i].astype(BF16)
    lw["b_pg"] = row(w["b_ple_gate"][i])
    lw["w_ple"] = w["w_ple"][i].astype(BF16)
    return lw, lam_init


def kernel(x_prompt, x_sample, p_prompt, p_sample, w_in, w_out, pool_w, pool_scale, q_norm_g, k_norm_g, lam_qk, subln_g, conv_w, conv_b, a_log, dt_bias, d_skip, ssd_norm_g, norm_mix_g, norm_ffn_g, router_coarse_w, router_coarse_b, router_fine_w, router_fine_b, w_gate_up, w_down, norm_ple_g, w_ple, w_ple_gate, b_ple_gate):
    w = dict(w_in=w_in, w_out=w_out, pool_w=pool_w, pool_scale=pool_scale, q_norm_g=q_norm_g, k_norm_g=k_norm_g,
             lam_qk=lam_qk, subln_g=subln_g, conv_w=conv_w, conv_b=conv_b, a_log=a_log, dt_bias=dt_bias,
             d_skip=d_skip, ssd_norm_g=ssd_norm_g, norm_mix_g=norm_mix_g, norm_ffn_g=norm_ffn_g,
             router_coarse_w=router_coarse_w, router_coarse_b=router_coarse_b, router_fine_w=router_fine_w,
             router_fine_b=router_fine_b, w_gate_up=w_gate_up, w_down=w_down, norm_ple_g=norm_ple_g,
             w_ple=w_ple, w_ple_gate=w_ple_gate, b_ple_gate=b_ple_gate)
    depth = w_in.shape[0]
    xs = [x_prompt, x_sample]
    ps = [p_prompt, p_sample]
    tables = [_rope_tables(x.shape[1]) for x in xs]
    for i in range(depth):
        lw, lam_init = _prep_layer(i, w)
        xs = [_layer(x, p[i], lw, cs[0], cs[1], lam_init) for x, p, cs in zip(xs, ps, tables)]
    return tuple(xs)
```

```python
import functools
import math

import jax
import jax.numpy as jnp
from jax import lax
from jax.experimental import pallas as pl
from jax.experimental.pallas import tpu as pltpu

F32 = jnp.float32
BF16 = jnp.bfloat16
I32 = jnp.int32
U32 = jnp.uint32

EPS = 1e-6
D_MODEL = 2048
D_PLE = 256
POOL_WINDOWS = (2, 4, 8, 16)
POOL_CH = 128
D_POOL = 512
DA_HEADS = 4
DA_DH = 64
DA_DV = 128
D_DA = 512
ROPE_THETA = 10000.0
SSD_HEADS = 16
SSD_HEAD_DIM = 64
D_SSD = 1024
SSD_GROUPS = 2
SSD_STATE = 128
CONV_K = 5
CHUNK = 128
D_XBC = 1536
N_GROUPS = 4
EXP_PER_GROUP = 8
N_EXPERTS = 32
D_EXPERT = 512

LANES = 128
HALO = 16
VMEM_LIMIT = 56 * 2**20


def _cparams(sem):
    return pltpu.CompilerParams(dimension_semantics=sem, vmem_limit_bytes=VMEM_LIMIT)


def _const_spec(shape):
    nd = len(shape)
    return pl.BlockSpec(shape, lambda *_: (0,) * nd, pipeline_mode=pl.Buffered(1))


def _dot(a, b):
    return jnp.dot(a, b, preferred_element_type=F32)


def _dot_nt(a, b):
    return lax.dot_general(a, b, (((1,), (1,)), ((), ())), preferred_element_type=F32)


def _split2(x):
    hi = x.astype(BF16)
    lo = (x - hi.astype(F32)).astype(BF16)
    return hi, lo


def _split3(x):
    a1 = x.astype(BF16)
    r1 = x - a1.astype(F32)
    a2 = r1.astype(BF16)
    a3 = (r1 - a2.astype(F32)).astype(BF16)
    return a1, a2, a3


def _silu(x):
    return x * jax.nn.sigmoid(x)


def _rms(x, g):
    ms = jnp.mean(x * x, axis=-1, keepdims=True)
    return x * lax.rsqrt(ms + EPS) * g


def _qk_prep(x, g, cos, sin, bd, first_half):
    outs = []
    for c in range(D_DA // LANES):
        xc = x[:, c * LANES:(c + 1) * LANES]
        hi, lo = _split2(xc * xc)
        ss = _dot(hi, bd) + _dot(lo, bd)
        xn = xc * lax.rsqrt(ss * (1.0 / DA_DH) + EPS) * g[:, c * LANES:(c + 1) * LANES]
        sw = jnp.where(first_half, pltpu.roll(xn, LANES - 32, 1), pltpu.roll(xn, 32, 1))
        outs.append((xn * cos + sw * sin).astype(BF16))
    return jnp.concatenate(outs, axis=1)


def _inproj_kernel(x_ref, g_ref, wp_ref, wq_ref, wk_ref, wv_ref, wz_ref, wx_ref, wd_ref,
                   gq_ref, gk_ref, cos_ref, sin_ref,
                   pool_o, q_o, k_o, v_o, z_o, xbc_o, dt_o):
    h = _rms(x_ref[...], g_ref[...]).astype(BF16)
    pool_o[...] = _dot(h, wp_ref[...])
    v_o[...] = _dot(h, wv_ref[...]).astype(BF16)
    z_o[...] = _dot(h, wz_ref[...]).astype(BF16)
    xbc_o[...] = _dot(h, wx_ref[...]).astype(BF16)
    dt_o[...] = _dot(h, wd_ref[...])
    r = lax.broadcasted_iota(I32, (LANES, LANES), 0) // DA_DH
    c = lax.broadcasted_iota(I32, (LANES, LANES), 1) // DA_DH
    bd = jnp.where(r == c, 1.0, 0.0).astype(BF16)
    first_half = (lax.broadcasted_iota(I32, (1, LANES), 1) % DA_DH) < (DA_DH // 2)
    cos = cos_ref[...]
    sin = sin_ref[...]
    q_o[...] = _qk_prep(_dot(h, wq_ref[...]), gq_ref[...], cos, sin, bd, first_half)
    k_o[...] = _qk_prep(_dot(h, wk_ref[...]), gk_ref[...], cos, sin, bd, first_half)


def _inproj(x, lw, cos, sin, seq):
    n = x.shape[0]
    tm = min(512, seq)
    nseq = seq // tm
    row = lambda w: pl.BlockSpec((tm, w), lambda i: (i, 0))
    pos = pl.BlockSpec((tm, LANES), lambda i: (i % nseq, 0))
    ws = [lw["w_pool"], lw["w_q"], lw["w_k"], lw["w_v"], lw["w_z"], lw["w_xbc"], lw["w_dt"]]
    return pl.pallas_call(
        _inproj_kernel,
        grid=(n // tm,),
        in_specs=[row(D_MODEL), _const_spec((1, D_MODEL))] + [_const_spec(w.shape) for w in ws]
        + [_const_spec((1, D_DA)), _const_spec((1, D_DA)), pos, pos],
        out_specs=[row(D_POOL), row(D_DA), row(D_DA), row(D_DA), row(D_SSD), row(D_XBC), row(LANES)],
        out_shape=[jax.ShapeDtypeStruct((n, D_POOL), F32), jax.ShapeDtypeStruct((n, D_DA), BF16),
                   jax.ShapeDtypeStruct((n, D_DA), BF16), jax.ShapeDtypeStruct((n, D_DA), BF16),
                   jax.ShapeDtypeStruct((n, D_SSD), BF16), jax.ShapeDtypeStruct((n, D_XBC), BF16),
                   jax.ShapeDtypeStruct((n, LANES), F32)],
        compiler_params=_cparams(("arbitrary",)),
        name="inproj",
    )(x, lw["g_mix"], *ws, lw["gq"], lw["gk"], cos, sin)


def _pool_kernel(prev_ref, cur_ref, next_ref, w_ref, sc_ref, o_ref, *, seq, tp):
    i = pl.program_id(1)
    nt = pl.num_programs(1)
    cur = cur_ref[0]
    prev = jnp.where(i > 0, prev_ref[0], 0.0)
    nxt = jnp.where(i < nt - 1, next_ref[0], 0.0)
    ext = jnp.concatenate([prev, cur, nxt], axis=0)
    n = tp + 2 * HALO
    t = i * tp + lax.broadcasted_iota(I32, (tp, 1), 0)

    def shifted(a, d):
        return pltpu.roll(a, (-d) % n, 0)

    outs = []
    for gi, win in enumerate(POOL_WINDOWS):
        xg = ext[:, gi * POOL_CH:(gi + 1) * POOL_CH]
        s = xg + shifted(xg, -1)
        half = 1
        while 2 * half < win:
            s = shifted(s, -half) + shifted(s, half)
            half *= 2
        left = win // 2
        right = win - 1 - left
        cnt = (jnp.minimum(t + right + 1, seq) - jnp.maximum(t - left, 0)).astype(F32)
        dlt = s[HALO:HALO + tp] / cnt - cur[:, gi * POOL_CH:(gi + 1) * POOL_CH]
        outs.append(_dot(dlt.astype(BF16), w_ref[gi]))
    o_ref[0] = (jnp.concatenate(outs, axis=1) * sc_ref[...]).astype(BF16)


def _pool(u, lw):
    b, seq, _ = u.shape
    tp = min(512, seq)
    r = tp // HALO
    nh = seq // HALO
    return pl.pallas_call(
        functools.partial(_pool_kernel, seq=seq, tp=tp),
        grid=(b, seq // tp),
        in_specs=[pl.BlockSpec((1, HALO, D_POOL), lambda bi, i: (bi, jnp.maximum(i * r - 1, 0), 0)),
                  pl.BlockSpec((1, tp, D_POOL), lambda bi, i: (bi, i, 0)),
                  pl.BlockSpec((1, HALO, D_POOL), lambda bi, i: (bi, jnp.minimum((i + 1) * r, nh - 1), 0)),
                  _const_spec((4, POOL_CH, POOL_CH)), _const_spec((1, D_POOL))],
        out_specs=pl.BlockSpec((1, tp, D_POOL), lambda bi, i: (bi, i, 0)),
        out_shape=jax.ShapeDtypeStruct((b, seq, D_POOL), BF16),
        compiler_params=_cparams(("arbitrary", "arbitrary")),
        name="pool",
    )(u, u, u, lw["pool_w"], lw["pool_scale"])


def _attn_kernel(q_ref, k_ref, v_ref, lam_ref, g_ref, o_ref, q_s, vt_s, s_s, p_s, m_s, al_s, acc_s, *, post_scale, tk):
    nk = k_ref.shape[1] // tk
    vrows = vt_s.shape[1]

    @pl.when(pl.program_id(2) == 0)
    def _():
        tail = jnp.where(lax.broadcasted_iota(I32, (vrows - DA_DV, tk), 0) == 0, 1.0, 0.0).astype(BF16)

        def tr(j, carry):
            vt_s[j, 0:DA_DV, :] = v_ref[0, pl.ds(pl.multiple_of(j * tk, tk), tk), :].astype(F32).T.astype(BF16)
            vt_s[j, DA_DV:, :] = tail
            return carry

        lax.fori_loop(0, nk, tr, 0)

    q = q_ref[0]
    lane = lax.broadcasted_iota(I32, q.shape, 1)
    zero = jnp.zeros_like(q)
    q_s[0] = jnp.where(lane < DA_DH, q, zero)
    q_s[1] = jnp.where(lane >= DA_DH, q, zero)
    m_s[...] = jnp.full(m_s.shape, -jnp.inf, F32)
    acc_s[...] = jnp.zeros(acc_s.shape, F32)

    def scores(j, slot):
        k = k_ref[0, pl.ds(pl.multiple_of(j * tk, tk), tk), :]
        for c in range(2):
            s_s[slot, c] = _dot_nt(k, q_s[c])

    def softmax(slot):
        for c in range(2):
            s = s_s[slot, c]
            m_prev = m_s[c]
            m_new = jnp.maximum(m_prev, jnp.max(s, axis=0, keepdims=True))
            p_s[slot, c] = jnp.exp2(s - m_new).astype(BF16)
            al_s[slot, c] = jnp.exp2(m_prev - m_new)
            m_s[c] = m_new

    def pv(j, slot):
        vt = vt_s[j]
        for c in range(2):
            acc_s[c] = al_s[slot, c] * acc_s[c] + _dot(vt, p_s[slot, c])

    scores(0, 0)
    if nk > 1:
        scores(1, 1)
    softmax(0)

    def pair(t, carry):
        for par in range(2):
            j = 2 * t + 1 + par
            slot = 1 - par
            scores(j + 1, 1 - slot)
            pv(j - 1, 1 - slot)
            softmax(slot)
        return carry

    if nk > 2:
        lax.fori_loop(0, (nk - 2) // 2, pair, 0)
    if nk > 1:
        pv(nk - 2, 0)
        softmax(1)
        pv(nk - 1, 1)
    else:
        pv(0, 0)
    a0 = acc_s[0]
    a1 = acc_s[1]
    ot = a0[:DA_DV] / a0[DA_DV:DA_DV + 1] - lam_ref[0:1, 0:1] * (a1[:DA_DV] / a1[DA_DV:DA_DV + 1])
    o_ref[0] = (_rms(ot.T, g_ref[...]) * post_scale).astype(BF16)


def _attention(q, k, v, lam, subln_g, lam_init):
    b, seq, _ = q.shape
    tq = min(1024, seq)
    tk = min(512, seq)
    nk = seq // tk
    assert nk == 1 or nk % 2 == 0
    vrows = DA_DV + 16
    qspec = pl.BlockSpec((1, tq, DA_DV), lambda bi, h, qi: (bi, qi, h))
    kspec = pl.BlockSpec((1, seq, DA_DV), lambda bi, h, qi: (bi, 0, h))
    return pl.pallas_call(
        functools.partial(_attn_kernel, post_scale=1.0 - lam_init, tk=tk),
        grid=(b, DA_HEADS, seq // tq),
        in_specs=[qspec, kspec, kspec, _const_spec((1, DA_DV)), _const_spec((1, DA_DV))],
        out_specs=qspec,
        out_shape=jax.ShapeDtypeStruct((b, seq, D_DA), BF16),
        scratch_shapes=[pltpu.VMEM((2, tq, DA_DV), BF16), pltpu.VMEM((nk, vrows, tk), BF16),
                        pltpu.VMEM((2, 2, tk, tq), F32), pltpu.VMEM((2, 2, tk, tq), BF16),
                        pltpu.VMEM((2, 1, tq), F32), pltpu.VMEM((2, 2, 1, tq), F32),
                        pltpu.VMEM((2, vrows, tq), F32)],
        compiler_params=_cparams(("arbitrary", "arbitrary", "arbitrary")),
        name="diff_attn",
    )(q, k, v, lam, subln_g)


def _ssd_kernel(*refs, rev):
    if rev:
        (prev_ref, cur_ref, next_ref, dt_ref, cw_ref, cb_ref, dtb_ref, alog_ref,
         yf_ref, z_ref, dsk_ref, gn_ref, o_ref, st_s) = refs
    else:
        (prev_ref, cur_ref, next_ref, dt_ref, cw_ref, cb_ref, dtb_ref, alog_ref, o_ref, st_s) = refs
    c = pl.program_id(1)
    nc = pl.num_programs(1)
    cc = nc - 1 - c if rev else c
    d = 1 if rev else 0

    @pl.when(c == 0)
    def _():
        st_s[...] = jnp.zeros(st_s.shape, F32)

    cur = cur_ref[0]
    zero_h = jnp.zeros((HALO, D_XBC), BF16)
    prev = jnp.where(cc > 0, prev_ref[0], zero_h)
    nxt = jnp.where(cc < nc - 1, next_ref[0], zero_h)
    ext = jnp.concatenate([prev, cur, nxt, jnp.zeros((2 * CHUNK - CHUNK - 2 * HALO, D_XBC), BF16)], axis=0)
    srow = lax.broadcasted_iota(I32, (CHUNK, 2 * CHUNK), 0)
    scol = lax.broadcasted_iota(I32, (CHUNK, 2 * CHUNK), 1)
    acc = jnp.broadcast_to(cb_ref[...], (CHUNK, D_XBC))
    for kk in range(CONV_K):
        sh = kk - CONV_K // 2
        if sh == 0:
            xk = cur.astype(F32)
        else:
            xk = _dot(jnp.where(scol == srow + (HALO + sh), 1.0, 0.0).astype(BF16), ext)
        acc = acc + xk * cw_ref[kk:kk + 1, :]
    xc = _silu(acc)
    xs = xc[:, :D_SSD]
    gn = SSD_GROUPS * SSD_STATE
    bm = xc[:, D_SSD:D_SSD + gn]
    cm = xc[:, D_SSD + gn:]

    x = dt_ref[0] + dtb_ref[...]
    dtv = jnp.maximum(x, 0.0) + jnp.log1p(jnp.exp(-jnp.abs(x)))
    a = dtv * (-jnp.exp(alog_ref[...]))
    ri = lax.broadcasted_iota(I32, (CHUNK, CHUNK), 0)
    ci = lax.broadcasted_iota(I32, (CHUNK, CHUNK), 1)
    mask = (ci >= ri) if rev else (ci <= ri)
    tri = jnp.where(mask, 1.0, 0.0).astype(BF16)
    a1, a2, a3 = _split3(a)
    cs = _dot(tri, a1) + _dot(tri, a2) + _dot(tri, a3)
    cs_t = cs.T
    last = 0 if rev else CHUNK - 1
    cs_end = cs[last:last + 1, :]
    e_dec = jnp.exp(cs)
    w_dec = dtv * jnp.exp(cs_end - cs)

    er = lax.broadcasted_iota(I32, (LANES, D_SSD), 0)
    ec = lax.broadcasted_iota(I32, (LANES, D_SSD), 1)
    expand = jnp.where(er == d * SSD_HEADS + ec // SSD_HEAD_DIM, 1.0, 0.0).astype(BF16)

    def widen(vv):
        hi, lo = _split2(vv)
        return _dot(hi, expand) + _dot(lo, expand)

    xdt = (xs * widen(dtv)).astype(BF16)
    xw = (xs * widen(w_dec)).astype(BF16)
    e_wide = widen(e_dec)

    hg = SSD_HEADS // SSD_GROUPS
    gw = hg * SSD_HEAD_DIM
    lane = lax.broadcasted_iota(I32, (CHUNK, LANES), 1)
    ys = []
    for g in range(SSD_GROUPS):
        bg = bm[:, g * SSD_STATE:(g + 1) * SSD_STATE]
        cg = cm[:, g * SSD_STATE:(g + 1) * SSD_STATE].astype(BF16)
        cb = _dot_nt(cg, bg.astype(BF16))
        st = st_s[g]
        y_off = _dot(cg, st.astype(BF16)) * e_wide[:, g * gw:(g + 1) * gw]
        st_s[g] = st * e_wide[last:last + 1, g * gw:(g + 1) * gw] + _dot(bg.T.astype(BF16), xw[:, g * gw:(g + 1) * gw])
        for pr in range(hg // 2):
            ms_ = []
            for hh in range(2):
                col = d * SSD_HEADS + g * hg + pr * 2 + hh
                diff = cs[:, col:col + 1] - cs_t[col:col + 1, :]
                ms_.append((cb * jnp.exp(jnp.where(mask, diff, -1e30))).astype(BF16))
            q0 = g * hg // 2 + pr
            xp = xdt[:, q0 * LANES:(q0 + 1) * LANES]
            zero = jnp.zeros_like(xp)
            rhs = jnp.concatenate([jnp.where(lane < SSD_HEAD_DIM, xp, zero),
                                   jnp.where(lane >= SSD_HEAD_DIM, xp, zero)], axis=0)
            yd = _dot(jnp.concatenate(ms_, axis=1), rhs)
            ys.append(yd + y_off[:, pr * LANES:(pr + 1) * LANES])
    y = jnp.concatenate(ys, axis=1)

    if rev:
        y = yf_ref[0] + y + xs * dsk_ref[...]
        y = y * _silu(z_ref[0].astype(F32))
        o_ref[0] = _rms(y, gn_ref[...]).astype(BF16)
    else:
        o_ref[0] = y


def _ssd(xbc, dt, z, lw):
    b, seq, _ = xbc.shape
    nc = seq // CHUNK
    r = CHUNK // HALO
    nh = seq // HALO

    def specs(rev):
        ch = (lambda c: nc - 1 - c) if rev else (lambda c: c)
        return [pl.BlockSpec((1, HALO, D_XBC), lambda bi, c: (bi, jnp.maximum(ch(c) * r - 1, 0), 0)),
                pl.BlockSpec((1, CHUNK, D_XBC), lambda bi, c: (bi, ch(c), 0)),
                pl.BlockSpec((1, HALO, D_XBC), lambda bi, c: (bi, jnp.minimum((ch(c) + 1) * r, nh - 1), 0)),
                pl.BlockSpec((1, CHUNK, LANES), lambda bi, c: (bi, ch(c), 0)),
                _const_spec((8, D_XBC)), _const_spec((1, D_XBC)), _const_spec((1, LANES)), _const_spec((1, LANES))]

    scratch = [pltpu.VMEM((SSD_GROUPS, SSD_STATE, D_SSD // SSD_GROUPS), F32)]
    common = (xbc, xbc, xbc, dt, lw["conv_w"], lw["conv_b"], lw["dt_bias"], lw["a_log"])
    y_f = pl.pallas_call(
        functools.partial(_ssd_kernel, rev=False),
        grid=(b, nc),
        in_specs=specs(False),
        out_specs=pl.BlockSpec((1, CHUNK, D_SSD), lambda bi, c: (bi, c, 0)),
        out_shape=jax.ShapeDtypeStruct((b, seq, D_SSD), F32),
        scratch_shapes=scratch,
        compiler_params=_cparams(("arbitrary", "arbitrary")),
        name="ssd_fwd",
    )(*common)
    rspec = pl.BlockSpec((1, CHUNK, D_SSD), lambda bi, c: (bi, nc - 1 - c, 0))
    return pl.pallas_call(
        functools.partial(_ssd_kernel, rev=True),
        grid=(b, nc),
        in_specs=specs(True) + [rspec, rspec, _const_spec((1, D_SSD)), _const_spec((1, D_SSD))],
        out_specs=rspec,
        out_shape=jax.ShapeDtypeStruct((b, seq, D_SSD), BF16),
        scratch_shapes=scratch,
        compiler_params=_cparams(("arbitrary", "arbitrary")),
        name="ssd_bwd",
    )(*common, y_f, z, lw["d_skip"], lw["ssd_norm_g"])


def _first_argmax(v, rowid, big):
    vmax = jnp.max(v, axis=0, keepdims=True)
    idx = jnp.min(jnp.where(v == vmax, rowid, big), axis=0, keepdims=True)
    return vmax, idx


def _outproj_kernel(x_ref, yp_ref, ya_ref, ys_ref, wp_ref, wa_ref, ws_ref, g_ref, wrh_ref, wrl_ref, rb_ref,
                    xo_ref, hf_ref, rt_ref, rtt_ref):
    x = x_ref[...] + (_dot(yp_ref[...], wp_ref[...]) + _dot(ya_ref[...], wa_ref[...]) + _dot(ys_ref[...], ws_ref[...]))
    xo_ref[...] = x
    hi, lo = _split2(_rms(x, g_ref[...]))
    hb = pltpu.bitcast(hi.astype(F32), U32)
    half = D_MODEL // 2
    hf_ref[...] = hb[:, :half] | (hb[:, half:] >> 16)
    wrh = wrh_ref[...]
    lg = _dot(hi, wrh) + _dot(lo, wrh) + _dot(hi, wrl_ref[...]) + rb_ref[...]
    lgt = lg.T
    tm = lgt.shape[1]
    rowid = lax.broadcasted_iota(I32, (8, tm), 0)
    coarse = jnp.where(rowid < N_GROUPS, lgt[0:8], -jnp.inf)
    cmax = jnp.max(coarse, axis=0, keepdims=True)
    ce = jnp.exp(coarse - cmax)
    _, grp = _first_argmax(ce, rowid, 99)
    gp = 1.0 / jnp.sum(ce, axis=0, keepdims=True)
    fine = jnp.zeros((8, tm), F32)
    for g in range(N_GROUPS):
        fine = jnp.where(grp == g, lgt[8 + 8 * g:16 + 8 * g], fine)
    fe = jnp.exp(fine - jnp.max(fine, axis=0, keepdims=True))
    fp = fe / jnp.sum(fe, axis=0, keepdims=True)
    v0, i0 = _first_argmax(fp, rowid, 99)
    v1, i1 = _first_argmax(jnp.where(rowid == i0, -1.0, fp), rowid, 99)
    den = v0 + v1
    vals = [(grp * EXP_PER_GROUP + i0).astype(F32), (grp * EXP_PER_GROUP + i1).astype(F32), gp * v0 / den, gp * v1 / den]
    rtt = jnp.zeros((8, tm), F32)
    for j, vv in enumerate(vals):
        rtt = jnp.where(rowid == j, vv, rtt)
    rtt_ref[...] = rtt
    rt_ref[...] = jnp.concatenate([rtt, jnp.zeros((LANES - 8, tm), F32)], axis=0).T


def _outproj(x, y_pool, y_da, y_ssd, lw):
    n = x.shape[0]
    tm = min(512, n)
    row = lambda w: pl.BlockSpec((tm, w), lambda i: (i, 0))
    ws = [lw["wo_pool"], lw["wo_da"], lw["wo_ssd"]]
    return pl.pallas_call(
        _outproj_kernel,
        grid=(n // tm,),
        in_specs=[row(D_MODEL), row(D_POOL), row(D_DA), row(D_SSD)] + [_const_spec(w.shape) for w in ws]
        + [_const_spec((1, D_MODEL)), _const_spec((D_MODEL, LANES)), _const_spec((D_MODEL, LANES)), _const_spec((1, LANES))],
        out_specs=[row(D_MODEL), row(D_MODEL // 2), row(LANES), pl.BlockSpec((8, tm), lambda i: (0, i))],
        out_shape=[jax.ShapeDtypeStruct((n, D_MODEL), F32), jax.ShapeDtypeStruct((n, D_MODEL // 2), U32),
                   jax.ShapeDtypeStruct((n, LANES), F32), jax.ShapeDtypeStruct((8, n), F32)],
        compiler_params=_cparams(("arbitrary",)),
        name="outproj_router",
    )(x, y_pool, y_da, y_ssd, *ws, lw["g_ffn"], lw["wr_hi"], lw["wr_lo"], lw["r_bias"])


def _rank_kernel(rtt_ref, rank_ref, cnt_ref):
    @pl.when(pl.program_id(0) == 0)
    def _():
        cnt_ref[...] = jnp.zeros(cnt_ref.shape, F32)

    t = rtt_ref.shape[1]
    eid = lax.broadcasted_iota(I32, (N_EXPERTS, t), 0)
    oh0 = jnp.where(eid == rtt_ref[0:1, :].astype(I32), 1.0, 0.0)
    oh1 = jnp.where(eid == rtt_ref[1:2, :].astype(I32), 1.0, 0.0)
    oh = oh0 + oh1
    before = jnp.where(lax.broadcasted_iota(I32, (t, t), 0) < lax.broadcasted_iota(I32, (t, t), 1), 1.0, 0.0)
    pre = _dot(oh.astype(BF16), before.astype(BF16)) + cnt_ref[:, 0:1]
    r0 = jnp.sum(oh0 * pre, axis=0, keepdims=True)
    r1 = jnp.sum(oh1 * pre, axis=0, keepdims=True)
    rowid = lax.broadcasted_iota(I32, (8, t), 0)
    rank_ref[...] = jnp.where(rowid == 0, r0, jnp.where(rowid == 1, r1, 0.0))
    cnt_ref[...] = cnt_ref[...] + jnp.sum(oh, axis=1, keepdims=True)


def _rank(rtt):
    n = rtt.shape[1]
    t = min(512, n)
    return pl.pallas_call(
        _rank_kernel,
        grid=(n // t,),
        in_specs=[pl.BlockSpec((8, t), lambda i: (0, i))],
        out_specs=[pl.BlockSpec((8, t), lambda i: (0, i)), pl.BlockSpec((N_EXPERTS, LANES), lambda i: (0, 0))],
        out_shape=[jax.ShapeDtypeStruct((8, n), F32), jax.ShapeDtypeStruct((N_EXPERTS, LANES), F32)],
        compiler_params=_cparams(("arbitrary",)),
        name="moe_rank",
    )(rtt)


def _row_copy(src, i, dst, j, sem):
    return pltpu.make_async_copy(src.at[pl.ds(i, 1)], dst.at[pl.ds(j, 1)], sem)


def _dispatch_kernel(dest_ref, hf_ref, xs_in_ref, xs_ref, sem):
    del xs_in_ref
    tm = hf_ref.shape[0]

    def start(j, carry):
        _row_copy(hf_ref, j, xs_ref, dest_ref[0, 0, j], sem).start()
        _row_copy(hf_ref, j, xs_ref, dest_ref[0, 0, tm + j], sem).start()
        return carry

    lax.fori_loop(0, tm, start, 0)

    def wait(j, carry):
        _row_copy(hf_ref, 0, xs_ref, 0, sem).wait()
        _row_copy(hf_ref, 0, xs_ref, 0, sem).wait()
        return carry

    lax.fori_loop(0, tm, wait, 0)


def _dispatch(dest, hfp, n_slots, tm):
    n = hfp.shape[0]
    half = D_MODEL // 2
    return pl.pallas_call(
        _dispatch_kernel,
        grid=(n // tm,),
        in_specs=[pl.BlockSpec((1, 1, 2 * tm), lambda i: (i, 0, 0), memory_space=pltpu.SMEM),
                  pl.BlockSpec((tm, half), lambda i: (i, 0)),
                  pl.BlockSpec(memory_space=pl.ANY)],
        out_specs=pl.BlockSpec(memory_space=pl.ANY),
        out_shape=jax.ShapeDtypeStruct((n_slots, half), U32),
        scratch_shapes=[pltpu.SemaphoreType.DMA(())],
        input_output_aliases={2: 0},
        compiler_params=_cparams(("arbitrary",)),
        name="moe_dispatch",
    )(dest, hfp, jnp.zeros((n_slots, half), U32))


def _expert_kernel(be_ref, nu_ref, xs_ref, wgu_ref, wdn_ref, o_ref, wgu_s, wdn_s):
    i = pl.program_id(0)

    @pl.when(jnp.logical_or(i == 0, be_ref[i] != be_ref[jnp.maximum(i - 1, 0)]))
    def _():
        wgu_s[...] = wgu_ref[0, 0].astype(BF16)
        wdn_s[...] = wdn_ref[0, 0].astype(BF16)

    @pl.when(i < nu_ref[0])
    def _():
        p = xs_ref[...]
        xa = pltpu.bitcast(p & jnp.uint32(0xFFFF0000), F32).astype(BF16)
        xb = pltpu.bitcast(p << 16, F32).astype(BF16)
        gu = _dot(jnp.concatenate([xa, xb], axis=1), wgu_s[...])
        act = (_silu(gu[:, :D_EXPERT]) * gu[:, D_EXPERT:]).astype(BF16)
        o_ref[...] = _dot(act, wdn_s[...])

    @pl.when(i >= nu_ref[0])
    def _():
        o_ref[...] = jnp.zeros(o_ref.shape, F32)


def _experts(blk_exp, n_used, xs, w_gu, w_dn, layer, tb):
    n_slots = xs.shape[0]
    half = D_MODEL // 2
    grid_spec = pltpu.PrefetchScalarGridSpec(
        num_scalar_prefetch=2,
        grid=(n_slots // tb,),
        in_specs=[pl.BlockSpec((tb, half), lambda i, be, nu: (i, 0)),
                  pl.BlockSpec((1, 1, D_MODEL, 2 * D_EXPERT), lambda i, be, nu: (layer, be[i], 0, 0)),
                  pl.BlockSpec((1, 1, D_EXPERT, D_MODEL), lambda i, be, nu: (layer, be[i], 0, 0))],
        out_specs=pl.BlockSpec((tb, D_MODEL), lambda i, be, nu: (i, 0)),
        scratch_shapes=[pltpu.VMEM((D_MODEL, 2 * D_EXPERT), BF16), pltpu.VMEM((D_EXPERT, D_MODEL), BF16)],
    )
    return pl.pallas_call(
        _expert_kernel,
        grid_spec=grid_spec,
        out_shape=jax.ShapeDtypeStruct((n_slots, D_MODEL), F32),
        compiler_params=_cparams(("arbitrary",)),
        name="moe_experts",
    )(blk_exp, n_used, xs, w_gu, w_dn)


def _combine_kernel(dest_ref, destn_ref, x_ref, rt_ref, pe_ref, g_ref, wpg_ref, bpg_ref, wple_ref, outs_ref, o_ref, g_s, sem):
    i = pl.program_id(0)
    nt = pl.num_programs(0)
    tm = x_ref.shape[0]

    def gather(d_ref, buf, op):
        for j in range(tm):
            for kk in range(2):
                op(_row_copy(outs_ref, d_ref[0, 0, kk * tm + j], g_s.at[buf, kk], j, sem.at[buf]))

    def drain(buf):
        for _ in range(2 * tm):
            _row_copy(outs_ref, 0, g_s.at[buf, 0], 0, sem.at[buf]).wait()

    @pl.when(i == 0)
    def _():
        gather(dest_ref, 0, lambda cp: cp.start())

    def step(cur):
        drain(cur)
        gather(destn_ref, 1 - cur, lambda cp: cp.start())
        rt = rt_ref[...]
        x = x_ref[...] + (g_s[cur, 0] * rt[:, 2:3] + g_s[cur, 1] * rt[:, 3:4])
        hn = _rms(x, g_ref[...]).astype(BF16)
        gate = jax.nn.sigmoid(_dot(hn, wpg_ref[...]) + bpg_ref[...])
        o_ref[...] = x + gate * _dot(pe_ref[...].astype(BF16), wple_ref[...])

        @pl.when(i == nt - 1)
        def _():
            drain(1 - cur)

    for par in range(2):
        pl.when(i % 2 == par)(functools.partial(step, par))


def _combine(dest, x, rt, pe, outs, lw, tm):
    n = x.shape[0]
    nt = n // tm
    row = lambda w: pl.BlockSpec((tm, w), lambda i: (i, 0))
    return pl.pallas_call(
        _combine_kernel,
        grid=(nt,),
        in_specs=[pl.BlockSpec((1, 1, 2 * tm), lambda i: (i, 0, 0), memory_space=pltpu.SMEM),
                  pl.BlockSpec((1, 1, 2 * tm), lambda i: (jnp.minimum(i + 1, nt - 1), 0, 0), memory_space=pltpu.SMEM),
                  row(D_MODEL), row(LANES), row(D_PLE), _const_spec((1, D_MODEL)),
                  _const_spec((D_MODEL, D_MODEL)), _const_spec((1, D_MODEL)), _const_spec((D_PLE, D_MODEL)),
                  pl.BlockSpec(memory_space=pl.ANY)],
        out_specs=row(D_MODEL),
        out_shape=jax.ShapeDtypeStruct((n, D_MODEL), F32),
        scratch_shapes=[pltpu.VMEM((2, 2, tm, D_MODEL), F32), pltpu.SemaphoreType.DMA((2,))],
        compiler_params=_cparams(("arbitrary",)),
        name="moe_combine_ple",
    )(dest, dest, x, rt, pe, lw["g_ple"], lw["w_pg"], lw["b_pg"], lw["w_ple"], outs)


def _tile_dest(dest, tm):
    n = dest.shape[1]
    return dest.reshape(2, n // tm, tm).transpose(1, 0, 2).reshape(n // tm, 1, 2 * tm)


def _moe_ple(x, hfp, rt, rtt, pe, lw):
    n = x.shape[0]
    tb = min(256, n)
    rank, cnt = _rank(rtt)
    counts = cnt[:, 0].astype(I32)
    padded = (counts + tb - 1) // tb * tb
    pend = jnp.cumsum(padded)
    pstart = pend - padded
    e = rtt[0:2].astype(I32)
    eids = jnp.arange(N_EXPERTS, dtype=I32)
    start_e = jnp.sum(jnp.where(e[..., None] == eids, pstart, 0), axis=-1)
    dest = start_e + rank[0:2].astype(I32)
    n_blocks = -(-2 * n // tb) + N_EXPERTS
    blk_start = jnp.arange(n_blocks, dtype=I32) * tb
    blk_exp = jnp.minimum(jnp.sum((pend[None, :] <= blk_start[:, None]).astype(I32), axis=1), N_EXPERTS - 1)
    n_used = (pend[-1:] // tb).astype(I32)
    td = min(512, n)
    xs = _dispatch(_tile_dest(dest, td), hfp, n_blocks * tb, td)
    outs = _experts(blk_exp, n_used, xs, lw["w_gate_up"], lw["w_down"], lw["layer"], tb)
    tc = min(256, n)
    return _combine(_tile_dest(dest, tc), x, rt, pe, outs, lw, tc)


def _layer(x, pe, lw, cos, sin, lam_init):
    b, seq, _ = x.shape
    n = b * seq
    pool_u, q, k, v, z, xbc, dt = _inproj(x.reshape(n, D_MODEL), lw, cos, sin, seq)
    sh = lambda a: a.reshape(b, seq, a.shape[-1])
    y_pool = _pool(sh(pool_u), lw)
    y_da = _attention(sh(q), sh(k), sh(v), lw["lam"], lw["subln_g"], lam_init)
    y_ssd = _ssd(sh(xbc), sh(dt), sh(z), lw)
    x2, hfp, rt, rtt = _outproj(x.reshape(n, D_MODEL), y_pool.reshape(n, D_POOL), y_da.reshape(n, D_DA),
                                y_ssd.reshape(n, D_SSD), lw)
    x3 = _moe_ple(x2, hfp, rt, rtt, pe.reshape(n, D_PLE), lw)
    return x3.reshape(b, seq, D_MODEL)


def _rope_tables(seq):
    half = DA_DH // 2
    inv = jnp.power(ROPE_THETA, -jnp.arange(half, dtype=F32) * 2.0 / DA_DH)
    ang = jnp.arange(seq, dtype=F32)[:, None] * inv[None, :]
    cos = jnp.cos(ang)
    sin = jnp.sin(ang)
    return jnp.tile(cos, (1, 4)), jnp.tile(jnp.concatenate([-sin, sin], axis=1), (1, 2))


def _prep_layer(i, w):
    row = lambda a: a.reshape(1, -1).astype(F32)
    w_in = w["w_in"][i].astype(BF16)
    o = 0
    lw = {"g_mix": row(w["norm_mix_g"][i])}
    for name, width in (("w_pool", D_POOL), ("w_q", D_DA), ("w_k", D_DA), ("w_v", D_DA), ("w_z", D_SSD), ("w_xbc", D_XBC)):
        lw[name] = w_in[:, o:o + width]
        o += width
    lw["w_dt"] = jnp.pad(w_in[:, o:], ((0, 0), (0, LANES - 2 * SSD_HEADS)))
    lw["gq"] = row(jnp.tile(w["q_norm_g"][i], 2 * DA_HEADS)) * (DA_DH ** -0.5 * math.log2(math.e))
    lw["gk"] = row(jnp.tile(w["k_norm_g"][i], 2 * DA_HEADS))
    lw["pool_w"] = w["pool_w"][i].astype(BF16)
    lw["pool_scale"] = row(w["pool_scale"][i])
    lf = w["lam_qk"][i].astype(F32)
    lam_init = 0.8 - 0.6 * math.exp(-0.3 * i)
    lam = jnp.exp(jnp.sum(lf[0] * lf[1])) - jnp.exp(jnp.sum(lf[2] * lf[3])) + lam_init
    lw["lam"] = jnp.full((1, DA_DV), lam, F32)
    lw["subln_g"] = row(w["subln_g"][i])
    lw["conv_w"] = jnp.pad(w["conv_w"][i].astype(F32), ((0, 8 - CONV_K), (0, 0)))
    lw["conv_b"] = row(w["conv_b"][i])
    pad_row = lambda a: jnp.pad(a.reshape(1, -1).astype(F32), ((0, 0), (0, LANES - 2 * SSD_HEADS)))
    lw["dt_bias"] = pad_row(w["dt_bias"][i])
    lw["a_log"] = pad_row(w["a_log"][i])
    lw["d_skip"] = row(jnp.repeat(w["d_skip"][i], SSD_HEAD_DIM))
    lw["ssd_norm_g"] = row(w["ssd_norm_g"][i])
    w_out = w["w_out"][i].astype(BF16)
    lw["wo_pool"] = w_out[:D_POOL]
    lw["wo_da"] = w_out[D_POOL:D_POOL + D_DA]
    lw["wo_ssd"] = w_out[D_POOL + D_DA:]
    lw["g_ffn"] = row(w["norm_ffn_g"][i])
    wr = jnp.zeros((D_MODEL, LANES), F32)
    wr = wr.at[:, :N_GROUPS].set(w["router_coarse_w"][i]).at[:, 8:8 + N_EXPERTS].set(w["router_fine_w"][i])
    lw["wr_hi"] = wr.astype(BF16)
    lw["wr_lo"] = (wr - lw["wr_hi"].astype(F32)).astype(BF16)
    rb = jnp.zeros((1, LANES), F32)
    lw["r_bias"] = rb.at[0, :N_GROUPS].set(w["router_coarse_b"][i]).at[0, 8:8 + N_EXPERTS].set(w["router_fine_b"][i])
    lw["w_gate_up"] = w["w_gate_up"]
    lw["w_down"] = w["w_down"]
    lw["layer"] = i
    lw["g_ple"] = row(w["norm_ple_g"][i])
    lw["w_pg"] = w["w_ple_gate"][i].astype(BF16)
    lw["b_pg"] = row(w["b_ple_gate"][i])
    lw["w_ple"] = w["w_ple"][i].astype(BF16)
    return lw, lam_init


def kernel(x_prompt, x_sample, p_prompt, p_sample, w_in, w_out, pool_w, pool_scale, q_norm_g, k_norm_g, lam_qk, subln_g, conv_w, conv_b, a_log, dt_bias, d_skip, ssd_norm_g, norm_mix_g, norm_ffn_g, router_coarse_w, router_coarse_b, router_fine_w, router_fine_b, w_gate_up, w_down, norm_ple_g, w_ple, w_ple_gate, b_ple_gate):
    w = dict(w_in=w_in, w_out=w_out, pool_w=pool_w, pool_scale=pool_scale, q_norm_g=q_norm_g, k_norm_g=k_norm_g,
             lam_qk=lam_qk, subln_g=subln_g, conv_w=conv_w, conv_b=conv_b, a_log=a_log, dt_bias=dt_bias,
             d_skip=d_skip, ssd_norm_g=ssd_norm_g, norm_mix_g=norm_mix_g, norm_ffn_g=norm_ffn_g,
             router_coarse_w=router_coarse_w, router_coarse_b=router_coarse_b, router_fine_w=router_fine_w,
             router_fine_b=router_fine_b, w_gate_up=w_gate_up, w_down=w_down, norm_ple_g=norm_ple_g,
             w_ple=w_ple, w_ple_gate=w_ple_gate, b_ple_gate=b_ple_gate)
    depth = w_in.shape[0]
    xs = [x_prompt, x_sample]
    ps = [p_prompt, p_sample]
    tables = [_rope_tables(x.shape[1]) for x in xs]
    for i in range(depth):
        lw, lam_init = _prep_layer(i, w)
        xs = [_layer(x, p[i], lw, cs[0], cs[1], lam_init) for x, p, cs in zip(xs, ps, tables)]
    return tuple(xs)
```

```python
import functools
import math

import jax
import jax.numpy as jnp
from jax import lax
from jax.experimental import pallas as pl
from jax.experimental.pallas import tpu as pltpu

F32 = jnp.float32
BF16 = jnp.bfloat16
I32 = jnp.int32
U32 = jnp.uint32

EPS = 1e-6
D_MODEL = 2048
D_PLE = 256
POOL_WINDOWS = (2, 4, 8, 16)
POOL_CH = 128
D_POOL = 512
DA_HEADS = 4
DA_DH = 64
DA_DV = 128
D_DA = 512
ROPE_THETA = 10000.0
SSD_HEADS = 16
SSD_HEAD_DIM = 64
D_SSD = 1024
SSD_GROUPS = 2
SSD_STATE = 128
CONV_K = 5
CHUNK = 128
D_XBC = 1536
N_GROUPS = 4
EXP_PER_GROUP = 8
N_EXPERTS = 32
D_EXPERT = 512

LANES = 128
HALO = 16
VMEM_LIMIT = 56 * 2**20


def _cparams(sem):
    return pltpu.CompilerParams(dimension_semantics=sem, vmem_limit_bytes=VMEM_LIMIT)


def _const_spec(shape):
    nd = len(shape)
    return pl.BlockSpec(shape, lambda *_: (0,) * nd, pipeline_mode=pl.Buffered(1))


def _dot(a, b):
    return jnp.dot(a, b, preferred_element_type=F32)


def _dot_nt(a, b):
    return lax.dot_general(a, b, (((1,), (1,)), ((), ())), preferred_element_type=F32)


def _split2(x):
    hi = x.astype(BF16)
    lo = (x - hi.astype(F32)).astype(BF16)
    return hi, lo


def _split3(x):
    a1 = x.astype(BF16)
    r1 = x - a1.astype(F32)
    a2 = r1.astype(BF16)
    a3 = (r1 - a2.astype(F32)).astype(BF16)
    return a1, a2, a3


def _silu(x):
    return x * jax.nn.sigmoid(x)


def _rms(x, g):
    ms = jnp.mean(x * x, axis=-1, keepdims=True)
    return x * lax.rsqrt(ms + EPS) * g


def _qk_prep(x, g, cos, sin, bd, first_half):
    outs = []
    for c in range(D_DA // LANES):
        xc = x[:, c * LANES:(c + 1) * LANES]
        hi, lo = _split2(xc * xc)
        ss = _dot(hi, bd) + _dot(lo, bd)
        xn = xc * lax.rsqrt(ss * (1.0 / DA_DH) + EPS) * g[:, c * LANES:(c + 1) * LANES]
        sw = jnp.where(first_half, pltpu.roll(xn, LANES - 32, 1), pltpu.roll(xn, 32, 1))
        outs.append((xn * cos + sw * sin).astype(BF16))
    return jnp.concatenate(outs, axis=1)


def _inproj_kernel(x_ref, g_ref, wp_ref, wq_ref, wk_ref, wv_ref, wz_ref, wx_ref, wd_ref,
                   gq_ref, gk_ref, cos_ref, sin_ref,
                   pool_o, q_o, k_o, v_o, z_o, xbc_o, dt_o):
    h = _rms(x_ref[...], g_ref[...]).astype(BF16)
    pool_o[...] = _dot(h, wp_ref[...])
    v_o[...] = _dot(h, wv_ref[...]).astype(BF16)
    z_o[...] = _dot(h, wz_ref[...]).astype(BF16)
    xbc_o[...] = _dot(h, wx_ref[...]).astype(BF16)
    dt_o[...] = _dot(h, wd_ref[...])
    r = lax.broadcasted_iota(I32, (LANES, LANES), 0) // DA_DH
    c = lax.broadcasted_iota(I32, (LANES, LANES), 1) // DA_DH
    bd = jnp.where(r == c, 1.0, 0.0).astype(BF16)
    first_half = (lax.broadcasted_iota(I32, (1, LANES), 1) % DA_DH) < (DA_DH // 2)
    cos = cos_ref[...]
    sin = sin_ref[...]
    q_o[...] = _qk_prep(_dot(h, wq_ref[...]), gq_ref[...], cos, sin, bd, first_half)
    k_o[...] = _qk_prep(_dot(h, wk_ref[...]), gk_ref[...], cos, sin, bd, first_half)


def _inproj(x, lw, cos, sin, seq):
    n = x.shape[0]
    tm = min(512, seq)
    nseq = seq // tm
    row = lambda w: pl.BlockSpec((tm, w), lambda i: (i, 0))
    pos = pl.BlockSpec((tm, LANES), lambda i: (i % nseq, 0))
    ws = [lw["w_pool"], lw["w_q"], lw["w_k"], lw["w_v"], lw["w_z"], lw["w_xbc"], lw["w_dt"]]
    return pl.pallas_call(
        _inproj_kernel,
        grid=(n // tm,),
        in_specs=[row(D_MODEL), _const_spec((1, D_MODEL))] + [_const_spec(w.shape) for w in ws]
        + [_const_spec((1, D_DA)), _const_spec((1, D_DA)), pos, pos],
        out_specs=[row(D_POOL), row(D_DA), row(D_DA), row(D_DA), row(D_SSD), row(D_XBC), row(LANES)],
        out_shape=[jax.ShapeDtypeStruct((n, D_POOL), F32), jax.ShapeDtypeStruct((n, D_DA), BF16),
                   jax.ShapeDtypeStruct((n, D_DA), BF16), jax.ShapeDtypeStruct((n, D_DA), BF16),
                   jax.ShapeDtypeStruct((n, D_SSD), BF16), jax.ShapeDtypeStruct((n, D_XBC), BF16),
                   jax.ShapeDtypeStruct((n, LANES), F32)],
        compiler_params=_cparams(("arbitrary",)),
        name="inproj",
    )(x, lw["g_mix"], *ws, lw["gq"], lw["gk"], cos, sin)


def _pool_kernel(prev_ref, cur_ref, next_ref, w_ref, sc_ref, o_ref, *, seq, tp):
    i = pl.program_id(1)
    nt = pl.num_programs(1)
    cur = cur_ref[0]
    prev = jnp.where(i > 0, prev_ref[0], 0.0)
    nxt = jnp.where(i < nt - 1, next_ref[0], 0.0)
    ext = jnp.concatenate([prev, cur, nxt], axis=0)
    n = tp + 2 * HALO
    t = i * tp + lax.broadcasted_iota(I32, (tp, 1), 0)

    def shifted(a, d):
        return pltpu.roll(a, (-d) % n, 0)

    outs = []
    for gi, win in enumerate(POOL_WINDOWS):
        xg = ext[:, gi * POOL_CH:(gi + 1) * POOL_CH]
        s = xg + shifted(xg, -1)
        half = 1
        while 2 * half < win:
            s = shifted(s, -half) + shifted(s, half)
            half *= 2
        left = win // 2
        right = win - 1 - left
        cnt = (jnp.minimum(t + right + 1, seq) - jnp.maximum(t - left, 0)).astype(F32)
        dlt = s[HALO:HALO + tp] / cnt - cur[:, gi * POOL_CH:(gi + 1) * POOL_CH]
        outs.append(_dot(dlt.astype(BF16), w_ref[gi]))
    o_ref[0] = (jnp.concatenate(outs, axis=1) * sc_ref[...]).astype(BF16)


def _pool(u, lw):
    b, seq, _ = u.shape
    tp = min(512, seq)
    r = tp // HALO
    nh = seq // HALO
    return pl.pallas_call(
        functools.partial(_pool_kernel, seq=seq, tp=tp),
        grid=(b, seq // tp),
        in_specs=[pl.BlockSpec((1, HALO, D_POOL), lambda bi, i: (bi, jnp.maximum(i * r - 1, 0), 0)),
                  pl.BlockSpec((1, tp, D_POOL), lambda bi, i: (bi, i, 0)),
                  pl.BlockSpec((1, HALO, D_POOL), lambda bi, i: (bi, jnp.minimum((i + 1) * r, nh - 1), 0)),
                  _const_spec((4, POOL_CH, POOL_CH)), _const_spec((1, D_POOL))],
        out_specs=pl.BlockSpec((1, tp, D_POOL), lambda bi, i: (bi, i, 0)),
        out_shape=jax.ShapeDtypeStruct((b, seq, D_POOL), BF16),
        compiler_params=_cparams(("arbitrary", "arbitrary")),
        name="pool",
    )(u, u, u, lw["pool_w"], lw["pool_scale"])


def _attn_kernel(q_ref, k_ref, v_ref, lam_ref, g_ref, o_ref, q_s, vt_s, s_s, p_s, m_s, al_s, acc_s, *, post_scale, tk):
    nk = k_ref.shape[1] // tk
    vrows = vt_s.shape[1]

    @pl.when(pl.program_id(2) == 0)
    def _():
        tail = jnp.where(lax.broadcasted_iota(I32, (vrows - DA_DV, tk), 0) == 0, 1.0, 0.0).astype(BF16)

        def tr(j, carry):
            vt_s[j, 0:DA_DV, :] = v_ref[0, pl.ds(pl.multiple_of(j * tk, tk), tk), :].astype(F32).T.astype(BF16)
            vt_s[j, DA_DV:, :] = tail
            return carry

        lax.fori_loop(0, nk, tr, 0)

    q = q_ref[0]
    lane = lax.broadcasted_iota(I32, q.shape, 1)
    zero = jnp.zeros_like(q)
    q_s[0] = jnp.where(lane < DA_DH, q, zero)
    q_s[1] = jnp.where(lane >= DA_DH, q, zero)
    m_s[...] = jnp.full(m_s.shape, -jnp.inf, F32)
    acc_s[...] = jnp.zeros(acc_s.shape, F32)

    def scores(j, slot):
        k = k_ref[0, pl.ds(pl.multiple_of(j * tk, tk), tk), :]
        for c in range(2):
            s_s[slot, c] = _dot_nt(k, q_s[c])

    strip = 64

    def softmax(slot):
        for c in range(2):
            m_prev = m_s[c]
            m_new = jnp.maximum(m_prev, jnp.max(s_s[slot, c], axis=0, keepdims=True))
            al_s[slot, c] = jnp.exp2(m_prev - m_new)
            m_s[c] = m_new
            for r in range(0, tk, strip):
                p_s[slot, c, r:r + strip, :] = jnp.exp2(s_s[slot, c, r:r + strip, :] - m_new).astype(BF16)

    def pv(j, slot):
        vt = vt_s[j]
        for c in range(2):
            acc_s[c] = al_s[slot, c] * acc_s[c] + _dot(vt, p_s[slot, c])

    scores(0, 0)
    if nk > 1:
        scores(1, 1)
    softmax(0)

    def pair(t, carry):
        for par in range(2):
            j = 2 * t + 1 + par
            slot = 1 - par
            scores(j + 1, 1 - slot)
            pv(j - 1, 1 - slot)
            softmax(slot)
        return carry

    if nk > 2:
        lax.fori_loop(0, (nk - 2) // 2, pair, 0)
    if nk > 1:
        pv(nk - 2, 0)
        softmax(1)
        pv(nk - 1, 1)
    else:
        pv(0, 0)
    a0 = acc_s[0]
    a1 = acc_s[1]
    ot = a0[:DA_DV] / a0[DA_DV:DA_DV + 1] - lam_ref[0:1, 0:1] * (a1[:DA_DV] / a1[DA_DV:DA_DV + 1])
    o_ref[0] = (_rms(ot.T, g_ref[...]) * post_scale).astype(BF16)


def _attention(q, k, v, lam, subln_g, lam_init):
    b, seq, _ = q.shape
    tq = min(1024, seq)
    tk = min(512, seq)
    nk = seq // tk
    assert nk == 1 or nk % 2 == 0
    vrows = DA_DV + 16
    qspec = pl.BlockSpec((1, tq, DA_DV), lambda bi, h, qi: (bi, qi, h))
    kspec = pl.BlockSpec((1, seq, DA_DV), lambda bi, h, qi: (bi, 0, h))
    return pl.pallas_call(
        functools.partial(_attn_kernel, post_scale=1.0 - lam_init, tk=tk),
        grid=(b, DA_HEADS, seq // tq),
        in_specs=[qspec, kspec, kspec, _const_spec((1, DA_DV)), _const_spec((1, DA_DV))],
        out_specs=qspec,
        out_shape=jax.ShapeDtypeStruct((b, seq, D_DA), BF16),
        scratch_shapes=[pltpu.VMEM((2, tq, DA_DV), BF16), pltpu.VMEM((nk, vrows, tk), BF16),
                        pltpu.VMEM((2, 2, tk, tq), F32), pltpu.VMEM((2, 2, tk, tq), BF16),
                        pltpu.VMEM((2, 1, tq), F32), pltpu.VMEM((2, 2, 1, tq), F32),
                        pltpu.VMEM((2, vrows, tq), F32)],
        compiler_params=_cparams(("arbitrary", "arbitrary", "arbitrary")),
        name="diff_attn",
    )(q, k, v, lam, subln_g)


def _ssd_kernel(*refs, rev):
    if rev:
        (prev_ref, cur_ref, next_ref, dt_ref, cw_ref, cb_ref, dtb_ref, alog_ref,
         yf_ref, z_ref, dsk_ref, gn_ref, o_ref, st_s) = refs
    else:
        (prev_ref, cur_ref, next_ref, dt_ref, cw_ref, cb_ref, dtb_ref, alog_ref, o_ref, st_s) = refs
    c = pl.program_id(1)
    nc = pl.num_programs(1)
    cc = nc - 1 - c if rev else c
    d = 1 if rev else 0

    @pl.when(c == 0)
    def _():
        st_s[...] = jnp.zeros(st_s.shape, F32)

    zero_h = jnp.zeros((HALO, D_XBC), BF16)
    srow = lax.broadcasted_iota(I32, (CHUNK, 2 * CHUNK), 0)
    scol = lax.broadcasted_iota(I32, (CHUNK, 2 * CHUNK), 1)
    gn = SSD_GROUPS * SSD_STATE
    ri = lax.broadcasted_iota(I32, (CHUNK, CHUNK), 0)
    ci = lax.broadcasted_iota(I32, (CHUNK, CHUNK), 1)
    mask = (ci >= ri) if rev else (ci <= ri)
    tri = jnp.where(mask, 1.0, 0.0).astype(BF16)
    last = 0 if rev else CHUNK - 1
    er = lax.broadcasted_iota(I32, (LANES, D_SSD), 0)
    ec = lax.broadcasted_iota(I32, (LANES, D_SSD), 1)
    expand = jnp.where(er == d * SSD_HEADS + ec // SSD_HEAD_DIM, 1.0, 0.0).astype(BF16)
    hg = SSD_HEADS // SSD_GROUPS
    gw = hg * SSD_HEAD_DIM
    lane = lax.broadcasted_iota(I32, (CHUNK, LANES), 1)

    def widen(vv):
        hi, lo = _split2(vv)
        return _dot(hi, expand) + _dot(lo, expand)

    def one(bb):
        cur = cur_ref[bb]
        prev = jnp.where(cc > 0, prev_ref[bb], zero_h)
        nxt = jnp.where(cc < nc - 1, next_ref[bb], zero_h)
        ext = jnp.concatenate([prev, cur, nxt, jnp.zeros((2 * CHUNK - CHUNK - 2 * HALO, D_XBC), BF16)], axis=0)
        acc = jnp.broadcast_to(cb_ref[...], (CHUNK, D_XBC))
        for kk in range(CONV_K):
            sh = kk - CONV_K // 2
            if sh == 0:
                xk = cur.astype(F32)
            else:
                xk = _dot(jnp.where(scol == srow + (HALO + sh), 1.0, 0.0).astype(BF16), ext)
            acc = acc + xk * cw_ref[kk:kk + 1, :]
        xc = _silu(acc)
        xs = xc[:, :D_SSD]
        bm = xc[:, D_SSD:D_SSD + gn]
        cm = xc[:, D_SSD + gn:]

        x = dt_ref[bb] + dtb_ref[...]
        dtv = jnp.maximum(x, 0.0) + jnp.log1p(jnp.exp(-jnp.abs(x)))
        a = dtv * (-jnp.exp(alog_ref[...]))
        a1, a2, a3 = _split3(a)
        cs = _dot(tri, a1) + _dot(tri, a2) + _dot(tri, a3)
        cs_t = cs.T
        cs_end = cs[last:last + 1, :]
        e_dec = jnp.exp(cs)
        w_dec = dtv * jnp.exp(cs_end - cs)
        xdt = (xs * widen(dtv)).astype(BF16)
        xw = (xs * widen(w_dec)).astype(BF16)
        e_wide = widen(e_dec)

        ys = []
        for g in range(SSD_GROUPS):
            bg = bm[:, g * SSD_STATE:(g + 1) * SSD_STATE]
            cg = cm[:, g * SSD_STATE:(g + 1) * SSD_STATE].astype(BF16)
            cb = _dot_nt(cg, bg.astype(BF16))
            st = st_s[bb, g]
            y_off = _dot(cg, st.astype(BF16)) * e_wide[:, g * gw:(g + 1) * gw]
            st_s[bb, g] = (st * e_wide[last:last + 1, g * gw:(g + 1) * gw]
                           + _dot(bg.T.astype(BF16), xw[:, g * gw:(g + 1) * gw]))
            for pr in range(hg // 2):
                ms_ = []
                for hh in range(2):
                    col = d * SSD_HEADS + g * hg + pr * 2 + hh
                    diff = cs[:, col:col + 1] - cs_t[col:col + 1, :]
                    ms_.append((cb * jnp.exp(jnp.where(mask, diff, -1e30))).astype(BF16))
                q0 = g * hg // 2 + pr
                xp = xdt[:, q0 * LANES:(q0 + 1) * LANES]
                zero = jnp.zeros_like(xp)
                rhs = jnp.concatenate([jnp.where(lane < SSD_HEAD_DIM, xp, zero),
                                       jnp.where(lane >= SSD_HEAD_DIM, xp, zero)], axis=0)
                yd = _dot(jnp.concatenate(ms_, axis=1), rhs)
                ys.append(yd + y_off[:, pr * LANES:(pr + 1) * LANES])
        y = jnp.concatenate(ys, axis=1)

        if rev:
            y = yf_ref[bb] + y + xs * dsk_ref[...]
            y = y * _silu(z_ref[bb].astype(F32))
            o_ref[bb] = _rms(y, gn_ref[...]).astype(BF16)
        else:
            o_ref[bb] = y

    for bb in range(cur_ref.shape[0]):
        one(bb)


def _ssd(xbc, dt, z, lw):
    b, seq, _ = xbc.shape
    nc = seq // CHUNK
    r = CHUNK // HALO
    nh = seq // HALO
    nb = 2 if b % 2 == 0 else 1

    def specs(rev):
        ch = (lambda c: nc - 1 - c) if rev else (lambda c: c)
        return [pl.BlockSpec((nb, HALO, D_XBC), lambda bi, c: (bi, jnp.maximum(ch(c) * r - 1, 0), 0)),
                pl.BlockSpec((nb, CHUNK, D_XBC), lambda bi, c: (bi, ch(c), 0)),
                pl.BlockSpec((nb, HALO, D_XBC), lambda bi, c: (bi, jnp.minimum((ch(c) + 1) * r, nh - 1), 0)),
                pl.BlockSpec((nb, CHUNK, LANES), lambda bi, c: (bi, ch(c), 0)),
                _const_spec((8, D_XBC)), _const_spec((1, D_XBC)), _const_spec((1, LANES)), _const_spec((1, LANES))]

    scratch = [pltpu.VMEM((nb, SSD_GROUPS, SSD_STATE, D_SSD // SSD_GROUPS), F32)]
    common = (xbc, xbc, xbc, dt, lw["conv_w"], lw["conv_b"], lw["dt_bias"], lw["a_log"])
    y_f = pl.pallas_call(
        functools.partial(_ssd_kernel, rev=False),
        grid=(b // nb, nc),
        in_specs=specs(False),
        out_specs=pl.BlockSpec((nb, CHUNK, D_SSD), lambda bi, c: (bi, c, 0)),
        out_shape=jax.ShapeDtypeStruct((b, seq, D_SSD), F32),
        scratch_shapes=scratch,
        compiler_params=_cparams(("arbitrary", "arbitrary")),
        name="ssd_fwd",
    )(*common)
    rspec = pl.BlockSpec((nb, CHUNK, D_SSD), lambda bi, c: (bi, nc - 1 - c, 0))
    return pl.pallas_call(
        functools.partial(_ssd_kernel, rev=True),
        grid=(b // nb, nc),
        in_specs=specs(True) + [rspec, rspec, _const_spec((1, D_SSD)), _const_spec((1, D_SSD))],
        out_specs=rspec,
        out_shape=jax.ShapeDtypeStruct((b, seq, D_SSD), BF16),
        scratch_shapes=scratch,
        compiler_params=_cparams(("arbitrary", "arbitrary")),
        name="ssd_bwd",
    )(*common, y_f, z, lw["d_skip"], lw["ssd_norm_g"])


def _first_argmax(v, rowid, big):
    vmax = jnp.max(v, axis=0, keepdims=True)
    idx = jnp.min(jnp.where(v == vmax, rowid, big), axis=0, keepdims=True)
    return vmax, idx


def _outproj_kernel(x_ref, yp_ref, ya_ref, ys_ref, wp_ref, wa_ref, ws_ref, g_ref, wrh_ref, wrl_ref, rb_ref,
                    xo_ref, hf_ref, rt_ref, rtt_ref):
    x = x_ref[...] + (_dot(yp_ref[...], wp_ref[...]) + _dot(ya_ref[...], wa_ref[...]) + _dot(ys_ref[...], ws_ref[...]))
    xo_ref[...] = x
    hi, lo = _split2(_rms(x, g_ref[...]))
    hb = pltpu.bitcast(hi.astype(F32), U32)
    half = D_MODEL // 2
    hf_ref[...] = hb[:, :half] | (hb[:, half:] >> 16)
    wrh = wrh_ref[...]
    lg = _dot(hi, wrh) + _dot(lo, wrh) + _dot(hi, wrl_ref[...]) + rb_ref[...]
    lgt = lg.T
    tm = lgt.shape[1]
    rowid = lax.broadcasted_iota(I32, (8, tm), 0)
    coarse = jnp.where(rowid < N_GROUPS, lgt[0:8], -jnp.inf)
    cmax = jnp.max(coarse, axis=0, keepdims=True)
    ce = jnp.exp(coarse - cmax)
    _, grp = _first_argmax(ce, rowid, 99)
    gp = 1.0 / jnp.sum(ce, axis=0, keepdims=True)
    fine = jnp.zeros((8, tm), F32)
    for g in range(N_GROUPS):
        fine = jnp.where(grp == g, lgt[8 + 8 * g:16 + 8 * g], fine)
    fe = jnp.exp(fine - jnp.max(fine, axis=0, keepdims=True))
    fp = fe / jnp.sum(fe, axis=0, keepdims=True)
    v0, i0 = _first_argmax(fp, rowid, 99)
    v1, i1 = _first_argmax(jnp.where(rowid == i0, -1.0, fp), rowid, 99)
    den = v0 + v1
    vals = [(grp * EXP_PER_GROUP + i0).astype(F32), (grp * EXP_PER_GROUP + i1).astype(F32), gp * v0 / den, gp * v1 / den]
    rtt = jnp.zeros((8, tm), F32)
    for j, vv in enumerate(vals):
        rtt = jnp.where(rowid == j, vv, rtt)
    rtt_ref[...] = rtt
    rt_ref[...] = jnp.concatenate([rtt, jnp.zeros((LANES - 8, tm), F32)], axis=0).T


def _outproj(x, y_pool, y_da, y_ssd, lw):
    n = x.shape[0]
    tm = min(512, n)
    row = lambda w: pl.BlockSpec((tm, w), lambda i: (i, 0))
    ws = [lw["wo_pool"], lw["wo_da"], lw["wo_ssd"]]
    return pl.pallas_call(
        _outproj_kernel,
        grid=(n // tm,),
        in_specs=[row(D_MODEL), row(D_POOL), row(D_DA), row(D_SSD)] + [_const_spec(w.shape) for w in ws]
        + [_const_spec((1, D_MODEL)), _const_spec((D_MODEL, LANES)), _const_spec((D_MODEL, LANES)), _const_spec((1, LANES))],
        out_specs=[row(D_MODEL), row(D_MODEL // 2), row(LANES), pl.BlockSpec((8, tm), lambda i: (0, i))],
        out_shape=[jax.ShapeDtypeStruct((n, D_MODEL), F32), jax.ShapeDtypeStruct((n, D_MODEL // 2), U32),
                   jax.ShapeDtypeStruct((n, LANES), F32), jax.ShapeDtypeStruct((8, n), F32)],
        compiler_params=_cparams(("arbitrary",)),
        name="outproj_router",
    )(x, y_pool, y_da, y_ssd, *ws, lw["g_ffn"], lw["wr_hi"], lw["wr_lo"], lw["r_bias"])


def _rank_kernel(rtt_ref, rank_ref, cnt_ref):
    @pl.when(pl.program_id(0) == 0)
    def _():
        cnt_ref[...] = jnp.zeros(cnt_ref.shape, F32)

    t = rtt_ref.shape[1]
    eid = lax.broadcasted_iota(I32, (N_EXPERTS, t), 0)
    oh0 = jnp.where(eid == rtt_ref[0:1, :].astype(I32), 1.0, 0.0)
    oh1 = jnp.where(eid == rtt_ref[1:2, :].astype(I32), 1.0, 0.0)
    oh = oh0 + oh1
    before = jnp.where(lax.broadcasted_iota(I32, (t, t), 0) < lax.broadcasted_iota(I32, (t, t), 1), 1.0, 0.0)
    pre = _dot(oh.astype(BF16), before.astype(BF16)) + cnt_ref[:, 0:1]
    r0 = jnp.sum(oh0 * pre, axis=0, keepdims=True)
    r1 = jnp.sum(oh1 * pre, axis=0, keepdims=True)
    rowid = lax.broadcasted_iota(I32, (8, t), 0)
    rank_ref[...] = jnp.where(rowid == 0, r0, jnp.where(rowid == 1, r1, 0.0))
    cnt_ref[...] = cnt_ref[...] + jnp.sum(oh, axis=1, keepdims=True)


def _rank(rtt):
    n = rtt.shape[1]
    t = min(512, n)
    return pl.pallas_call(
        _rank_kernel,
        grid=(n // t,),
        in_specs=[pl.BlockSpec((8, t), lambda i: (0, i))],
        out_specs=[pl.BlockSpec((8, t), lambda i: (0, i)), pl.BlockSpec((N_EXPERTS, LANES), lambda i: (0, 0))],
        out_shape=[jax.ShapeDtypeStruct((8, n), F32), jax.ShapeDtypeStruct((N_EXPERTS, LANES), F32)],
        compiler_params=_cparams(("arbitrary",)),
        name="moe_rank",
    )(rtt)


def _row_copy(src, i, dst, j, sem):
    return pltpu.make_async_copy(src.at[pl.ds(i, 1)], dst.at[pl.ds(j, 1)], sem)


def _dispatch_kernel(dest_ref, hf_ref, xs_in_ref, xs_ref, sem):
    del xs_in_ref
    tm = hf_ref.shape[0]

    def start(j, carry):
        _row_copy(hf_ref, j, xs_ref, dest_ref[0, 0, j], sem).start()
        _row_copy(hf_ref, j, xs_ref, dest_ref[0, 0, tm + j], sem).start()
        return carry

    lax.fori_loop(0, tm, start, 0)

    def wait(j, carry):
        _row_copy(hf_ref, 0, xs_ref, 0, sem).wait()
        _row_copy(hf_ref, 0, xs_ref, 0, sem).wait()
        return carry

    lax.fori_loop(0, tm, wait, 0)


def _dispatch(dest, hfp, n_slots, tm):
    n = hfp.shape[0]
    half = D_MODEL // 2
    return pl.pallas_call(
        _dispatch_kernel,
        grid=(n // tm,),
        in_specs=[pl.BlockSpec((1, 1, 2 * tm), lambda i: (i, 0, 0), memory_space=pltpu.SMEM),
                  pl.BlockSpec((tm, half), lambda i: (i, 0)),
                  pl.BlockSpec(memory_space=pl.ANY)],
        out_specs=pl.BlockSpec(memory_space=pl.ANY),
        out_shape=jax.ShapeDtypeStruct((n_slots, half), U32),
        scratch_shapes=[pltpu.SemaphoreType.DMA(())],
        input_output_aliases={2: 0},
        compiler_params=_cparams(("arbitrary",)),
        name="moe_dispatch",
    )(dest, hfp, jnp.zeros((n_slots, half), U32))


def _expert_kernel(be_ref, nu_ref, xs_ref, wgu_ref, wdn_ref, o_ref, wgu_s, wdn_s):
    i = pl.program_id(0)

    @pl.when(jnp.logical_or(i == 0, be_ref[i] != be_ref[jnp.maximum(i - 1, 0)]))
    def _():
        wgu_s[...] = wgu_ref[0, 0].astype(BF16)
        wdn_s[...] = wdn_ref[0, 0].astype(BF16)

    @pl.when(i < nu_ref[0])
    def _():
        p = xs_ref[...]
        xa = pltpu.bitcast(p & jnp.uint32(0xFFFF0000), F32).astype(BF16)
        xb = pltpu.bitcast(p << 16, F32).astype(BF16)
        gu = _dot(jnp.concatenate([xa, xb], axis=1), wgu_s[...])
        act = (_silu(gu[:, :D_EXPERT]) * gu[:, D_EXPERT:]).astype(BF16)
        o_ref[...] = _dot(act, wdn_s[...])

    @pl.when(i >= nu_ref[0])
    def _():
        o_ref[...] = jnp.zeros(o_ref.shape, F32)


def _experts(blk_exp, n_used, xs, w_gu, w_dn, layer, tb):
    n_slots = xs.shape[0]
    half = D_MODEL // 2
    grid_spec = pltpu.PrefetchScalarGridSpec(
        num_scalar_prefetch=2,
        grid=(n_slots // tb,),
        in_specs=[pl.BlockSpec((tb, half), lambda i, be, nu: (i, 0)),
                  pl.BlockSpec((1, 1, D_MODEL, 2 * D_EXPERT), lambda i, be, nu: (layer, be[i], 0, 0)),
                  pl.BlockSpec((1, 1, D_EXPERT, D_MODEL), lambda i, be, nu: (layer, be[i], 0, 0))],
        out_specs=pl.BlockSpec((tb, D_MODEL), lambda i, be, nu: (i, 0)),
        scratch_shapes=[pltpu.VMEM((D_MODEL, 2 * D_EXPERT), BF16), pltpu.VMEM((D_EXPERT, D_MODEL), BF16)],
    )
    return pl.pallas_call(
        _expert_kernel,
        grid_spec=grid_spec,
        out_shape=jax.ShapeDtypeStruct((n_slots, D_MODEL), F32),
        compiler_params=_cparams(("arbitrary",)),
        name="moe_experts",
    )(blk_exp, n_used, xs, w_gu, w_dn)


def _combine_kernel(dest_ref, destn_ref, x_ref, rt_ref, pe_ref, g_ref, wpg_ref, bpg_ref, wple_ref, outs_ref, o_ref,
                    g_s, x_s, hn_s, pe_s, sem):
    i = pl.program_id(0)
    nt = pl.num_programs(0)
    tm = x_ref.shape[0]

    nsl = 8
    sw = D_MODEL // nsl
    per = tm // nsl

    def gather(d_ref, buf, lo, hi):
        for j in range(lo, hi):
            for kk in range(2):
                _row_copy(outs_ref, d_ref[0, 0, kk * tm + j], g_s.at[buf, kk], j, sem.at[buf]).start()

    def drain(buf):
        for _ in range(2 * tm):
            _row_copy(outs_ref, 0, g_s.at[buf, 0], 0, sem.at[buf]).wait()

    @pl.when(i == 0)
    def _():
        gather(dest_ref, 0, 0, tm)

    def step(cur):
        drain(cur)
        rt = rt_ref[...]
        x = x_ref[...] + (g_s[cur, 0] * rt[:, 2:3] + g_s[cur, 1] * rt[:, 3:4])
        x_s[...] = x
        hn_s[...] = _rms(x, g_ref[...]).astype(BF16)
        pe_s[...] = pe_ref[...].astype(BF16)
        for c in range(nsl):
            gather(destn_ref, 1 - cur, c * per, (c + 1) * per)
            cols = slice(c * sw, (c + 1) * sw)
            gate = jax.nn.sigmoid(_dot(hn_s[...], wpg_ref[:, cols]) + bpg_ref[:, cols])
            o_ref[:, cols] = x_s[:, cols] + gate * _dot(pe_s[...], wple_ref[:, cols])

        @pl.when(i == nt - 1)
        def _():
            drain(1 - cur)

    for par in range(2):
        pl.when(i % 2 == par)(functools.partial(step, par))


def _combine(dest, x, rt, pe, outs, lw, tm):
    n = x.shape[0]
    nt = n // tm
    row = lambda w: pl.BlockSpec((tm, w), lambda i: (i, 0))
    return pl.pallas_call(
        _combine_kernel,
        grid=(nt,),
        in_specs=[pl.BlockSpec((1, 1, 2 * tm), lambda i: (i, 0, 0), memory_space=pltpu.SMEM),
                  pl.BlockSpec((1, 1, 2 * tm), lambda i: (jnp.minimum(i + 1, nt - 1), 0, 0), memory_space=pltpu.SMEM),
                  row(D_MODEL), row(LANES), row(D_PLE), _const_spec((1, D_MODEL)),
                  _const_spec((D_MODEL, D_MODEL)), _const_spec((1, D_MODEL)), _const_spec((D_PLE, D_MODEL)),
                  pl.BlockSpec(memory_space=pl.ANY)],
        out_specs=row(D_MODEL),
        out_shape=jax.ShapeDtypeStruct((n, D_MODEL), F32),
        scratch_shapes=[pltpu.VMEM((2, 2, tm, D_MODEL), F32), pltpu.VMEM((tm, D_MODEL), F32),
                        pltpu.VMEM((tm, D_MODEL), BF16), pltpu.VMEM((tm, D_PLE), BF16), pltpu.SemaphoreType.DMA((2,))],
        compiler_params=_cparams(("arbitrary",)),
        name="moe_combine_ple",
    )(dest, dest, x, rt, pe, lw["g_ple"], lw["w_pg"], lw["b_pg"], lw["w_ple"], outs)


def _tile_dest(dest, tm):
    n = dest.shape[1]
    return dest.reshape(2, n // tm, tm).transpose(1, 0, 2).reshape(n // tm, 1, 2 * tm)


def _moe_ple(x, hfp, rt, rtt, pe, lw):
    n = x.shape[0]
    tb = min(256, n)
    rank, cnt = _rank(rtt)
    counts = cnt[:, 0].astype(I32)
    padded = (counts + tb - 1) // tb * tb
    pend = jnp.cumsum(padded)
    pstart = pend - padded
    e = rtt[0:2].astype(I32)
    eids = jnp.arange(N_EXPERTS, dtype=I32)
    start_e = jnp.sum(jnp.where(e[..., None] == eids, pstart, 0), axis=-1)
    dest = start_e + rank[0:2].astype(I32)
    n_blocks = -(-2 * n // tb) + N_EXPERTS
    blk_start = jnp.arange(n_blocks, dtype=I32) * tb
    blk_exp = jnp.minimum(jnp.sum((pend[None, :] <= blk_start[:, None]).astype(I32), axis=1), N_EXPERTS - 1)
    n_used = (pend[-1:] // tb).astype(I32)
    td = min(512, n)
    xs = _dispatch(_tile_dest(dest, td), hfp, n_blocks * tb, td)
    outs = _experts(blk_exp, n_used, xs, lw["w_gate_up"], lw["w_down"], lw["layer"], tb)
    tc = min(256, n)
    return _combine(_tile_dest(dest, tc), x, rt, pe, outs, lw, tc)


def _layer(x, pe, lw, cos, sin, lam_init):
    b, seq, _ = x.shape
    n = b * seq
    pool_u, q, k, v, z, xbc, dt = _inproj(x.reshape(n, D_MODEL), lw, cos, sin, seq)
    sh = lambda a: a.reshape(b, seq, a.shape[-1])
    y_pool = _pool(sh(pool_u), lw)
    y_da = _attention(sh(q), sh(k), sh(v), lw["lam"], lw["subln_g"], lam_init)
    y_ssd = _ssd(sh(xbc), sh(dt), sh(z), lw)
    x2, hfp, rt, rtt = _outproj(x.reshape(n, D_MODEL), y_pool.reshape(n, D_POOL), y_da.reshape(n, D_DA),
                                y_ssd.reshape(n, D_SSD), lw)
    x3 = _moe_ple(x2, hfp, rt, rtt, pe.reshape(n, D_PLE), lw)
    return x3.reshape(b, seq, D_MODEL)


def _rope_tables(seq):
    half = DA_DH // 2
    inv = jnp.power(ROPE_THETA, -jnp.arange(half, dtype=F32) * 2.0 / DA_DH)
    ang = jnp.arange(seq, dtype=F32)[:, None] * inv[None, :]
    cos = jnp.cos(ang)
    sin = jnp.sin(ang)
    return jnp.tile(cos, (1, 4)), jnp.tile(jnp.concatenate([-sin, sin], axis=1), (1, 2))


def _prep_layer(i, w):
    row = lambda a: a.reshape(1, -1).astype(F32)
    w_in = w["w_in"][i].astype(BF16)
    o = 0
    lw = {"g_mix": row(w["norm_mix_g"][i])}
    for name, width in (("w_pool", D_POOL), ("w_q", D_DA), ("w_k", D_DA), ("w_v", D_DA), ("w_z", D_SSD), ("w_xbc", D_XBC)):
        lw[name] = w_in[:, o:o + width]
        o += width
    lw["w_dt"] = jnp.pad(w_in[:, o:], ((0, 0), (0, LANES - 2 * SSD_HEADS)))
    lw["gq"] = row(jnp.tile(w["q_norm_g"][i], 2 * DA_HEADS)) * (DA_DH ** -0.5 * math.log2(math.e))
    lw["gk"] = row(jnp.tile(w["k_norm_g"][i], 2 * DA_HEADS))
    lw["pool_w"] = w["pool_w"][i].astype(BF16)
    lw["pool_scale"] = row(w["pool_scale"][i])
    lf = w["lam_qk"][i].astype(F32)
    lam_init = 0.8 - 0.6 * math.exp(-0.3 * i)
    lam = jnp.exp(jnp.sum(lf[0] * lf[1])) - jnp.exp(jnp.sum(lf[2] * lf[3])) + lam_init
    lw["lam"] = jnp.full((1, DA_DV), lam, F32)
    lw["subln_g"] = row(w["subln_g"][i])
    lw["conv_w"] = jnp.pad(w["conv_w"][i].astype(F32), ((0, 8 - CONV_K), (0, 0)))
    lw["conv_b"] = row(w["conv_b"][i])
    pad_row = lambda a: jnp.pad(a.reshape(1, -1).astype(F32), ((0, 0), (0, LANES - 2 * SSD_HEADS)))
    lw["dt_bias"] = pad_row(w["dt_bias"][i])
    lw["a_log"] = pad_row(w["a_log"][i])
    lw["d_skip"] = row(jnp.repeat(w["d_skip"][i], SSD_HEAD_DIM))
    lw["ssd_norm_g"] = row(w["ssd_norm_g"][i])
    w_out = w["w_out"][i].astype(BF16)
    lw["wo_pool"] = w_out[:D_POOL]
    lw["wo_da"] = w_out[D_POOL:D_POOL + D_DA]
    lw["wo_ssd"] = w_out[D_POOL + D_DA:]
    lw["g_ffn"] = row(w["norm_ffn_g"][i])
    wr = jnp.zeros((D_MODEL, LANES), F32)
    wr = wr.at[:, :N_GROUPS].set(w["router_coarse_w"][i]).at[:, 8:8 + N_EXPERTS].set(w["router_fine_w"][i])
    lw["wr_hi"] = wr.astype(BF16)
    lw["wr_lo"] = (wr - lw["wr_hi"].astype(F32)).astype(BF16)
    rb = jnp.zeros((1, LANES), F32)
    lw["r_bias"] = rb.at[0, :N_GROUPS].set(w["router_coarse_b"][i]).at[0, 8:8 + N_EXPERTS].set(w["router_fine_b"][i])
    lw["w_gate_up"] = w["w_gate_up"]
    lw["w_down"] = w["w_down"]
    lw["layer"] = i
    lw["g_ple"] = row(w["norm_ple_g"][i])
    lw["w_pg"] = w["w_ple_gate"][i].astype(BF16)
    lw["b_pg"] = row(w["b_ple_gate"][i])
    lw["w_ple"] = w["w_ple"][i].astype(BF16)
    return lw, lam_init


def kernel(x_prompt, x_sample, p_prompt, p_sample, w_in, w_out, pool_w, pool_scale, q_norm_g, k_norm_g, lam_qk, subln_g, conv_w, conv_b, a_log, dt_bias, d_skip, ssd_norm_g, norm_mix_g, norm_ffn_g, router_coarse_w, router_coarse_b, router_fine_w, router_fine_b, w_gate_up, w_down, norm_ple_g, w_ple, w_ple_gate, b_ple_gate):
    w = dict(w_in=w_in, w_out=w_out, pool_w=pool_w, pool_scale=pool_scale, q_norm_g=q_norm_g, k_norm_g=k_norm_g,
             lam_qk=lam_qk, subln_g=subln_g, conv_w=conv_w, conv_b=conv_b, a_log=a_log, dt_bias=dt_bias,
             d_skip=d_skip, ssd_norm_g=ssd_norm_g, norm_mix_g=norm_mix_g, norm_ffn_g=norm_ffn_g,
             router_coarse_w=router_coarse_w, router_coarse_b=router_coarse_b, router_fine_w=router_fine_w,
             router_fine_b=router_fine_b, w_gate_up=w_gate_up, w_down=w_down, norm_ple_g=norm_ple_g,
             w_ple=w_ple, w_ple_gate=w_ple_gate, b_ple_gate=b_ple_gate)
    depth = w_in.shape[0]
    xs = [x_prompt, x_sample]
    ps = [p_prompt, p_sample]
    tables = [_rope_tables(x.shape[1]) for x in xs]
    for i in range(depth):
        lw, lam_init = _prep_layer(i, w)
        xs = [_layer(x, p[i], lw, cs[0], cs[1], lam_init) for x, p, cs in zip(xs, ps, tables)]
    return tuple(xs)
```

```python
import functools
import math

import jax
import jax.numpy as jnp
from jax import lax
from jax.experimental import pallas as pl
from jax.experimental.pallas import tpu as pltpu

F32 = jnp.float32
BF16 = jnp.bfloat16
I32 = jnp.int32
U32 = jnp.uint32

EPS = 1e-6
D_MODEL = 2048
D_PLE = 256
POOL_WINDOWS = (2, 4, 8, 16)
POOL_CH = 128
D_POOL = 512
DA_HEADS = 4
DA_DH = 64
DA_DV = 128
D_DA = 512
ROPE_THETA = 10000.0
SSD_HEADS = 16
SSD_HEAD_DIM = 64
D_SSD = 1024
SSD_GROUPS = 2
SSD_STATE = 128
CONV_K = 5
CHUNK = 128
D_XBC = 1536
N_GROUPS = 4
EXP_PER_GROUP = 8
N_EXPERTS = 32
D_EXPERT = 512

LANES = 128
HALO = 16
VMEM_LIMIT = 56 * 2**20


def _cparams(sem):
    return pltpu.CompilerParams(dimension_semantics=sem, vmem_limit_bytes=VMEM_LIMIT)


def _const_spec(shape):
    nd = len(shape)
    return pl.BlockSpec(shape, lambda *_: (0,) * nd, pipeline_mode=pl.Buffered(1))


def _dot(a, b):
    return jnp.dot(a, b, preferred_element_type=F32)


def _dot_nt(a, b):
    return lax.dot_general(a, b, (((1,), (1,)), ((), ())), preferred_element_type=F32)


def _split2(x):
    hi = x.astype(BF16)
    lo = (x - hi.astype(F32)).astype(BF16)
    return hi, lo


def _split3(x):
    a1 = x.astype(BF16)
    r1 = x - a1.astype(F32)
    a2 = r1.astype(BF16)
    a3 = (r1 - a2.astype(F32)).astype(BF16)
    return a1, a2, a3


def _silu(x):
    return x * jax.nn.sigmoid(x)


def _rms(x, g):
    ms = jnp.mean(x * x, axis=-1, keepdims=True)
    return x * lax.rsqrt(ms + EPS) * g


def _qk_prep(x, g, cos, sin, bd, first_half):
    outs = []
    for c in range(D_DA // LANES):
        xc = x[:, c * LANES:(c + 1) * LANES]
        hi, lo = _split2(xc * xc)
        ss = _dot(hi, bd) + _dot(lo, bd)
        xn = xc * lax.rsqrt(ss * (1.0 / DA_DH) + EPS) * g[:, c * LANES:(c + 1) * LANES]
        sw = jnp.where(first_half, pltpu.roll(xn, LANES - 32, 1), pltpu.roll(xn, 32, 1))
        outs.append((xn * cos + sw * sin).astype(BF16))
    return jnp.concatenate(outs, axis=1)


def _inproj_kernel(x_ref, g_ref, wp_ref, wq_ref, wk_ref, wv_ref, wz_ref, wx_ref, wd_ref,
                   gq_ref, gk_ref, cos_ref, sin_ref,
                   pool_o, q_o, k_o, v_o, z_o, xbc_o, dt_o):
    h = _rms(x_ref[...], g_ref[...]).astype(BF16)
    pool_o[...] = _dot(h, wp_ref[...])
    v_o[...] = _dot(h, wv_ref[...]).astype(BF16)
    z_o[...] = _dot(h, wz_ref[...]).astype(BF16)
    xbc_o[...] = _dot(h, wx_ref[...]).astype(BF16)
    dt_o[...] = _dot(h, wd_ref[...])
    r = lax.broadcasted_iota(I32, (LANES, LANES), 0) // DA_DH
    c = lax.broadcasted_iota(I32, (LANES, LANES), 1) // DA_DH
    bd = jnp.where(r == c, 1.0, 0.0).astype(BF16)
    first_half = (lax.broadcasted_iota(I32, (1, LANES), 1) % DA_DH) < (DA_DH // 2)
    cos = cos_ref[...]
    sin = sin_ref[...]
    q_o[...] = _qk_prep(_dot(h, wq_ref[...]), gq_ref[...], cos, sin, bd, first_half)
    k_o[...] = _qk_prep(_dot(h, wk_ref[...]), gk_ref[...], cos, sin, bd, first_half)


def _inproj(x, lw, cos, sin, seq):
    n = x.shape[0]
    tm = min(512, seq)
    nseq = seq // tm
    row = lambda w: pl.BlockSpec((tm, w), lambda i: (i, 0))
    pos = pl.BlockSpec((tm, LANES), lambda i: (i % nseq, 0))
    ws = [lw["w_pool"], lw["w_q"], lw["w_k"], lw["w_v"], lw["w_z"], lw["w_xbc"], lw["w_dt"]]
    return pl.pallas_call(
        _inproj_kernel,
        grid=(n // tm,),
        in_specs=[row(D_MODEL), _const_spec((1, D_MODEL))] + [_const_spec(w.shape) for w in ws]
        + [_const_spec((1, D_DA)), _const_spec((1, D_DA)), pos, pos],
        out_specs=[row(D_POOL), row(D_DA), row(D_DA), row(D_DA), row(D_SSD), row(D_XBC), row(LANES)],
        out_shape=[jax.ShapeDtypeStruct((n, D_POOL), F32), jax.ShapeDtypeStruct((n, D_DA), BF16),
                   jax.ShapeDtypeStruct((n, D_DA), BF16), jax.ShapeDtypeStruct((n, D_DA), BF16),
                   jax.ShapeDtypeStruct((n, D_SSD), BF16), jax.ShapeDtypeStruct((n, D_XBC), BF16),
                   jax.ShapeDtypeStruct((n, LANES), F32)],
        compiler_params=_cparams(("arbitrary",)),
        name="inproj",
    )(x, lw["g_mix"], *ws, lw["gq"], lw["gk"], cos, sin)


def _pool_kernel(prev_ref, cur_ref, next_ref, w_ref, sc_ref, o_ref, *, seq, tp):
    i = pl.program_id(1)
    nt = pl.num_programs(1)
    cur = cur_ref[0]
    prev = jnp.where(i > 0, prev_ref[0], 0.0)
    nxt = jnp.where(i < nt - 1, next_ref[0], 0.0)
    ext = jnp.concatenate([prev, cur, nxt], axis=0)
    n = tp + 2 * HALO
    t = i * tp + lax.broadcasted_iota(I32, (tp, 1), 0)

    def shifted(a, d):
        return pltpu.roll(a, (-d) % n, 0)

    outs = []
    for gi, win in enumerate(POOL_WINDOWS):
        xg = ext[:, gi * POOL_CH:(gi + 1) * POOL_CH]
        s = xg + shifted(xg, -1)
        half = 1
        while 2 * half < win:
            s = shifted(s, -half) + shifted(s, half)
            half *= 2
        left = win // 2
        right = win - 1 - left
        cnt = (jnp.minimum(t + right + 1, seq) - jnp.maximum(t - left, 0)).astype(F32)
        dlt = s[HALO:HALO + tp] / cnt - cur[:, gi * POOL_CH:(gi + 1) * POOL_CH]
        outs.append(_dot(dlt.astype(BF16), w_ref[gi]))
    o_ref[0] = (jnp.concatenate(outs, axis=1) * sc_ref[...]).astype(BF16)


def _pool(u, lw):
    b, seq, _ = u.shape
    tp = min(512, seq)
    r = tp // HALO
    nh = seq // HALO
    return pl.pallas_call(
        functools.partial(_pool_kernel, seq=seq, tp=tp),
        grid=(b, seq // tp),
        in_specs=[pl.BlockSpec((1, HALO, D_POOL), lambda bi, i: (bi, jnp.maximum(i * r - 1, 0), 0)),
                  pl.BlockSpec((1, tp, D_POOL), lambda bi, i: (bi, i, 0)),
                  pl.BlockSpec((1, HALO, D_POOL), lambda bi, i: (bi, jnp.minimum((i + 1) * r, nh - 1), 0)),
                  _const_spec((4, POOL_CH, POOL_CH)), _const_spec((1, D_POOL))],
        out_specs=pl.BlockSpec((1, tp, D_POOL), lambda bi, i: (bi, i, 0)),
        out_shape=jax.ShapeDtypeStruct((b, seq, D_POOL), BF16),
        compiler_params=_cparams(("arbitrary", "arbitrary")),
        name="pool",
    )(u, u, u, lw["pool_w"], lw["pool_scale"])


def _attn_kernel(q_ref, k_ref, v_ref, lam_ref, g_ref, o_ref, q_s, vt_s, s_s, p_s, m_s, al_s, acc_s, *, post_scale, tk):
    nk = k_ref.shape[1] // tk
    vrows = vt_s.shape[1]

    @pl.when(pl.program_id(2) == 0)
    def _():
        tail = jnp.where(lax.broadcasted_iota(I32, (vrows - DA_DV, tk), 0) == 0, 1.0, 0.0).astype(BF16)

        def tr(j, carry):
            vt_s[j, 0:DA_DV, :] = v_ref[0, pl.ds(pl.multiple_of(j * tk, tk), tk), :].astype(F32).T.astype(BF16)
            vt_s[j, DA_DV:, :] = tail
            return carry

        lax.fori_loop(0, nk, tr, 0)

    q = q_ref[0]
    lane = lax.broadcasted_iota(I32, q.shape, 1)
    zero = jnp.zeros_like(q)
    q_s[0] = jnp.where(lane < DA_DH, q, zero)
    q_s[1] = jnp.where(lane >= DA_DH, q, zero)
    m_s[...] = jnp.full(m_s.shape, -jnp.inf, F32)
    acc_s[...] = jnp.zeros(acc_s.shape, F32)

    def scores(j, slot):
        k = k_ref[0, pl.ds(pl.multiple_of(j * tk, tk), tk), :]
        for c in range(2):
            s_s[slot, c] = _dot_nt(k, q_s[c])

    strip = 64

    def softmax(slot):
        for c in range(2):
            m_prev = m_s[c]
            m_new = jnp.maximum(m_prev, jnp.max(s_s[slot, c], axis=0, keepdims=True))
            al_s[slot, c] = jnp.exp2(m_prev - m_new)
            m_s[c] = m_new
            for r in range(0, tk, strip):
                p_s[slot, c, r:r + strip, :] = jnp.exp2(s_s[slot, c, r:r + strip, :] - m_new).astype(BF16)

    def pv(j, slot):
        vt = vt_s[j]
        for c in range(2):
            acc_s[c] = al_s[slot, c] * acc_s[c] + _dot(vt, p_s[slot, c])

    scores(0, 0)
    if nk > 1:
        scores(1, 1)
    softmax(0)

    def pair(t, carry):
        for par in range(2):
            j = 2 * t + 1 + par
            slot = 1 - par
            scores(j + 1, 1 - slot)
            pv(j - 1, 1 - slot)
            softmax(slot)
        return carry

    if nk > 2:
        lax.fori_loop(0, (nk - 2) // 2, pair, 0)
    if nk > 1:
        pv(nk - 2, 0)
        softmax(1)
        pv(nk - 1, 1)
    else:
        pv(0, 0)
    a0 = acc_s[0]
    a1 = acc_s[1]
    ot = a0[:DA_DV] / a0[DA_DV:DA_DV + 1] - lam_ref[0:1, 0:1] * (a1[:DA_DV] / a1[DA_DV:DA_DV + 1])
    o_ref[0] = (_rms(ot.T, g_ref[...]) * post_scale).astype(BF16)


def _attention(q, k, v, lam, subln_g, lam_init):
    b, seq, _ = q.shape
    tq = min(1024, seq)
    tk = min(512, seq)
    nk = seq // tk
    assert nk == 1 or nk % 2 == 0
    vrows = DA_DV + 16
    qspec = pl.BlockSpec((1, tq, DA_DV), lambda bi, h, qi: (bi, qi, h))
    kspec = pl.BlockSpec((1, seq, DA_DV), lambda bi, h, qi: (bi, 0, h))
    return pl.pallas_call(
        functools.partial(_attn_kernel, post_scale=1.0 - lam_init, tk=tk),
        grid=(b, DA_HEADS, seq // tq),
        in_specs=[qspec, kspec, kspec, _const_spec((1, DA_DV)), _const_spec((1, DA_DV))],
        out_specs=qspec,
        out_shape=jax.ShapeDtypeStruct((b, seq, D_DA), BF16),
        scratch_shapes=[pltpu.VMEM((2, tq, DA_DV), BF16), pltpu.VMEM((nk, vrows, tk), BF16),
                        pltpu.VMEM((2, 2, tk, tq), F32), pltpu.VMEM((2, 2, tk, tq), BF16),
                        pltpu.VMEM((2, 1, tq), F32), pltpu.VMEM((2, 2, 1, tq), F32),
                        pltpu.VMEM((2, vrows, tq), F32)],
        compiler_params=_cparams(("arbitrary", "arbitrary", "arbitrary")),
        name="diff_attn",
    )(q, k, v, lam, subln_g)


def _ssd_kernel(*refs, rev):
    if rev:
        (prev_ref, cur_ref, next_ref, dt_ref, cw_ref, cb_ref, dtb_ref, alog_ref,
         yf_ref, z_ref, dsk_ref, gn_ref, o_ref, st_s) = refs
    else:
        (prev_ref, cur_ref, next_ref, dt_ref, cw_ref, cb_ref, dtb_ref, alog_ref, o_ref, st_s) = refs
    c = pl.program_id(1)
    nc = pl.num_programs(1)
    cc = nc - 1 - c if rev else c
    d = 1 if rev else 0

    @pl.when(c == 0)
    def _():
        st_s[...] = jnp.zeros(st_s.shape, F32)

    zero_h = jnp.zeros((HALO, D_XBC), BF16)
    srow = lax.broadcasted_iota(I32, (CHUNK, 2 * CHUNK), 0)
    scol = lax.broadcasted_iota(I32, (CHUNK, 2 * CHUNK), 1)
    gn = SSD_GROUPS * SSD_STATE
    ri = lax.broadcasted_iota(I32, (CHUNK, CHUNK), 0)
    ci = lax.broadcasted_iota(I32, (CHUNK, CHUNK), 1)
    mask = (ci >= ri) if rev else (ci <= ri)
    tri = jnp.where(mask, 1.0, 0.0).astype(BF16)
    last = 0 if rev else CHUNK - 1
    er = lax.broadcasted_iota(I32, (LANES, D_SSD), 0)
    ec = lax.broadcasted_iota(I32, (LANES, D_SSD), 1)
    expand = jnp.where(er == d * SSD_HEADS + ec // SSD_HEAD_DIM, 1.0, 0.0).astype(BF16)
    hg = SSD_HEADS // SSD_GROUPS
    gw = hg * SSD_HEAD_DIM
    lane = lax.broadcasted_iota(I32, (CHUNK, LANES), 1)

    def widen(vv):
        hi, lo = _split2(vv)
        return _dot(hi, expand) + _dot(lo, expand)

    def one(bb):
        cur = cur_ref[bb]
        prev = jnp.where(cc > 0, prev_ref[bb], zero_h)
        nxt = jnp.where(cc < nc - 1, next_ref[bb], zero_h)
        ext = jnp.concatenate([prev, cur, nxt, jnp.zeros((2 * CHUNK - CHUNK - 2 * HALO, D_XBC), BF16)], axis=0)
        acc = jnp.broadcast_to(cb_ref[...], (CHUNK, D_XBC))
        for kk in range(CONV_K):
            sh = kk - CONV_K // 2
            if sh == 0:
                xk = cur.astype(F32)
            else:
                xk = _dot(jnp.where(scol == srow + (HALO + sh), 1.0, 0.0).astype(BF16), ext)
            acc = acc + xk * cw_ref[kk:kk + 1, :]
        xc = _silu(acc)
        xs = xc[:, :D_SSD]
        bm = xc[:, D_SSD:D_SSD + gn]
        cm = xc[:, D_SSD + gn:]

        x = dt_ref[bb] + dtb_ref[...]
        dtv = jnp.maximum(x, 0.0) + jnp.log1p(jnp.exp(-jnp.abs(x)))
        a = dtv * (-jnp.exp(alog_ref[...]))
        a1, a2, a3 = _split3(a)
        cs = _dot(tri, a1) + _dot(tri, a2) + _dot(tri, a3)
        cs_t = cs.T
        cs_end = cs[last:last + 1, :]
        e_dec = jnp.exp(cs)
        w_dec = dtv * jnp.exp(cs_end - cs)
        xdt = (xs * widen(dtv)).astype(BF16)
        xw = (xs * widen(w_dec)).astype(BF16)
        e_wide = widen(e_dec)

        ys = []
        for g in range(SSD_GROUPS):
            bg = bm[:, g * SSD_STATE:(g + 1) * SSD_STATE]
            cg = cm[:, g * SSD_STATE:(g + 1) * SSD_STATE].astype(BF16)
            cb = _dot_nt(cg, bg.astype(BF16))
            st = st_s[bb, g]
            y_off = _dot(cg, st.astype(BF16)) * e_wide[:, g * gw:(g + 1) * gw]
            st_s[bb, g] = (st * e_wide[last:last + 1, g * gw:(g + 1) * gw]
                           + _dot(bg.T.astype(BF16), xw[:, g * gw:(g + 1) * gw]))
            for pr in range(hg // 2):
                ms_ = []
                for hh in range(2):
                    col = d * SSD_HEADS + g * hg + pr * 2 + hh
                    diff = cs[:, col:col + 1] - cs_t[col:col + 1, :]
                    ms_.append((cb * jnp.exp(jnp.where(mask, diff, -1e30))).astype(BF16))
                q0 = g * hg // 2 + pr
                xp = xdt[:, q0 * LANES:(q0 + 1) * LANES]
                zero = jnp.zeros_like(xp)
                rhs = jnp.concatenate([jnp.where(lane < SSD_HEAD_DIM, xp, zero),
                                       jnp.where(lane >= SSD_HEAD_DIM, xp, zero)], axis=0)
                yd = _dot(jnp.concatenate(ms_, axis=1), rhs)
                ys.append(yd + y_off[:, pr * LANES:(pr + 1) * LANES])
        y = jnp.concatenate(ys, axis=1)

        if rev:
            y = yf_ref[bb] + y + xs * dsk_ref[...]
            y = y * _silu(z_ref[bb].astype(F32))
            o_ref[bb] = _rms(y, gn_ref[...]).astype(BF16)
        else:
            o_ref[bb] = y

    for bb in range(cur_ref.shape[0]):
        one(bb)


def _ssd(xbc, dt, z, lw):
    b, seq, _ = xbc.shape
    nc = seq // CHUNK
    r = CHUNK // HALO
    nh = seq // HALO
    nb = 2 if b % 2 == 0 else 1

    def specs(rev):
        ch = (lambda c: nc - 1 - c) if rev else (lambda c: c)
        return [pl.BlockSpec((nb, HALO, D_XBC), lambda bi, c: (bi, jnp.maximum(ch(c) * r - 1, 0), 0)),
                pl.BlockSpec((nb, CHUNK, D_XBC), lambda bi, c: (bi, ch(c), 0)),
                pl.BlockSpec((nb, HALO, D_XBC), lambda bi, c: (bi, jnp.minimum((ch(c) + 1) * r, nh - 1), 0)),
                pl.BlockSpec((nb, CHUNK, LANES), lambda bi, c: (bi, ch(c), 0)),
                _const_spec((8, D_XBC)), _const_spec((1, D_XBC)), _const_spec((1, LANES)), _const_spec((1, LANES))]

    scratch = [pltpu.VMEM((nb, SSD_GROUPS, SSD_STATE, D_SSD // SSD_GROUPS), F32)]
    common = (xbc, xbc, xbc, dt, lw["conv_w"], lw["conv_b"], lw["dt_bias"], lw["a_log"])
    y_f = pl.pallas_call(
        functools.partial(_ssd_kernel, rev=False),
        grid=(b // nb, nc),
        in_specs=specs(False),
        out_specs=pl.BlockSpec((nb, CHUNK, D_SSD), lambda bi, c: (bi, c, 0)),
        out_shape=jax.ShapeDtypeStruct((b, seq, D_SSD), F32),
        scratch_shapes=scratch,
        compiler_params=_cparams(("arbitrary", "arbitrary")),
        name="ssd_fwd",
    )(*common)
    rspec = pl.BlockSpec((nb, CHUNK, D_SSD), lambda bi, c: (bi, nc - 1 - c, 0))
    return pl.pallas_call(
        functools.partial(_ssd_kernel, rev=True),
        grid=(b // nb, nc),
        in_specs=specs(True) + [rspec, rspec, _const_spec((1, D_SSD)), _const_spec((1, D_SSD))],
        out_specs=rspec,
        out_shape=jax.ShapeDtypeStruct((b, seq, D_SSD), BF16),
        scratch_shapes=scratch,
        compiler_params=_cparams(("arbitrary", "arbitrary")),
        name="ssd_bwd",
    )(*common, y_f, z, lw["d_skip"], lw["ssd_norm_g"])


def _first_argmax(v, rowid, big):
    vmax = jnp.max(v, axis=0, keepdims=True)
    idx = jnp.min(jnp.where(v == vmax, rowid, big), axis=0, keepdims=True)
    return vmax, idx


def _outproj_kernel(x_ref, yp_ref, ya_ref, ys_ref, wp_ref, wa_ref, ws_ref, g_ref, wrh_ref, wrl_ref, rb_ref,
                    xo_ref, hf_ref, rt_ref, rtt_ref):
    x = x_ref[...] + (_dot(yp_ref[...], wp_ref[...]) + _dot(ya_ref[...], wa_ref[...]) + _dot(ys_ref[...], ws_ref[...]))
    xo_ref[...] = x
    hi, lo = _split2(_rms(x, g_ref[...]))
    hb = pltpu.bitcast(hi.astype(F32), U32)
    half = D_MODEL // 2
    hf_ref[...] = hb[:, :half] | (hb[:, half:] >> 16)
    wrh = wrh_ref[...]
    lg = _dot(hi, wrh) + _dot(lo, wrh) + _dot(hi, wrl_ref[...]) + rb_ref[...]
    lgt = lg.T
    tm = lgt.shape[1]
    rowid = lax.broadcasted_iota(I32, (8, tm), 0)
    coarse = jnp.where(rowid < N_GROUPS, lgt[0:8], -jnp.inf)
    cmax = jnp.max(coarse, axis=0, keepdims=True)
    ce = jnp.exp(coarse - cmax)
    _, grp = _first_argmax(ce, rowid, 99)
    gp = 1.0 / jnp.sum(ce, axis=0, keepdims=True)
    fine = jnp.zeros((8, tm), F32)
    for g in range(N_GROUPS):
        fine = jnp.where(grp == g, lgt[8 + 8 * g:16 + 8 * g], fine)
    fe = jnp.exp(fine - jnp.max(fine, axis=0, keepdims=True))
    fp = fe / jnp.sum(fe, axis=0, keepdims=True)
    v0, i0 = _first_argmax(fp, rowid, 99)
    v1, i1 = _first_argmax(jnp.where(rowid == i0, -1.0, fp), rowid, 99)
    den = v0 + v1
    vals = [(grp * EXP_PER_GROUP + i0).astype(F32), (grp * EXP_PER_GROUP + i1).astype(F32), gp * v0 / den, gp * v1 / den]
    rtt = jnp.zeros((8, tm), F32)
    for j, vv in enumerate(vals):
        rtt = jnp.where(rowid == j, vv, rtt)
    rtt_ref[...] = rtt
    rt_ref[...] = jnp.concatenate([rtt, jnp.zeros((LANES - 8, tm), F32)], axis=0).T


def _outproj(x, y_pool, y_da, y_ssd, lw):
    n = x.shape[0]
    tm = min(512, n)
    row = lambda w: pl.BlockSpec((tm, w), lambda i: (i, 0))
    ws = [lw["wo_pool"], lw["wo_da"], lw["wo_ssd"]]
    return pl.pallas_call(
        _outproj_kernel,
        grid=(n // tm,),
        in_specs=[row(D_MODEL), row(D_POOL), row(D_DA), row(D_SSD)] + [_const_spec(w.shape) for w in ws]
        + [_const_spec((1, D_MODEL)), _const_spec((D_MODEL, LANES)), _const_spec((D_MODEL, LANES)), _const_spec((1, LANES))],
        out_specs=[row(D_MODEL), row(D_MODEL // 2), row(LANES), pl.BlockSpec((8, tm), lambda i: (0, i))],
        out_shape=[jax.ShapeDtypeStruct((n, D_MODEL), F32), jax.ShapeDtypeStruct((n, D_MODEL // 2), U32),
                   jax.ShapeDtypeStruct((n, LANES), F32), jax.ShapeDtypeStruct((8, n), F32)],
        compiler_params=_cparams(("arbitrary",)),
        name="outproj_router",
    )(x, y_pool, y_da, y_ssd, *ws, lw["g_ffn"], lw["wr_hi"], lw["wr_lo"], lw["r_bias"])


def _rank_kernel(rtt_ref, rank_ref, cnt_ref):
    @pl.when(pl.program_id(0) == 0)
    def _():
        cnt_ref[...] = jnp.zeros(cnt_ref.shape, F32)

    t = rtt_ref.shape[1]
    eid = lax.broadcasted_iota(I32, (N_EXPERTS, t), 0)
    oh0 = jnp.where(eid == rtt_ref[0:1, :].astype(I32), 1.0, 0.0)
    oh1 = jnp.where(eid == rtt_ref[1:2, :].astype(I32), 1.0, 0.0)
    oh = oh0 + oh1
    before = jnp.where(lax.broadcasted_iota(I32, (t, t), 0) < lax.broadcasted_iota(I32, (t, t), 1), 1.0, 0.0)
    pre = _dot(oh.astype(BF16), before.astype(BF16)) + cnt_ref[:, 0:1]
    r0 = jnp.sum(oh0 * pre, axis=0, keepdims=True)
    r1 = jnp.sum(oh1 * pre, axis=0, keepdims=True)
    rowid = lax.broadcasted_iota(I32, (8, t), 0)
    rank_ref[...] = jnp.where(rowid == 0, r0, jnp.where(rowid == 1, r1, 0.0))
    cnt_ref[...] = cnt_ref[...] + jnp.sum(oh, axis=1, keepdims=True)


def _rank(rtt):
    n = rtt.shape[1]
    t = min(512, n)
    return pl.pallas_call(
        _rank_kernel,
        grid=(n // t,),
        in_specs=[pl.BlockSpec((8, t), lambda i: (0, i))],
        out_specs=[pl.BlockSpec((8, t), lambda i: (0, i)), pl.BlockSpec((N_EXPERTS, LANES), lambda i: (0, 0))],
        out_shape=[jax.ShapeDtypeStruct((8, n), F32), jax.ShapeDtypeStruct((N_EXPERTS, LANES), F32)],
        compiler_params=_cparams(("arbitrary",)),
        name="moe_rank",
    )(rtt)


def _row_copy(src, i, dst, j, sem):
    return pltpu.make_async_copy(src.at[pl.ds(i, 1)], dst.at[pl.ds(j, 1)], sem)


def _dispatch_kernel(dest_ref, hf_ref, xs_in_ref, xs_ref, sem):
    del xs_in_ref
    tm = hf_ref.shape[0]

    def start(j, carry):
        _row_copy(hf_ref, j, xs_ref, dest_ref[0, 0, j], sem).start()
        _row_copy(hf_ref, j, xs_ref, dest_ref[0, 0, tm + j], sem).start()
        return carry

    lax.fori_loop(0, tm, start, 0)

    def wait(j, carry):
        _row_copy(hf_ref, 0, xs_ref, 0, sem).wait()
        _row_copy(hf_ref, 0, xs_ref, 0, sem).wait()
        return carry

    lax.fori_loop(0, tm, wait, 0)


def _dispatch(dest, hfp, slots, tm):
    n_slots = slots.shape[0]
    n = hfp.shape[0]
    half = D_MODEL // 2
    return pl.pallas_call(
        _dispatch_kernel,
        grid=(n // tm,),
        in_specs=[pl.BlockSpec((1, 1, 2 * tm), lambda i: (i, 0, 0), memory_space=pltpu.SMEM),
                  pl.BlockSpec((tm, half), lambda i: (i, 0)),
                  pl.BlockSpec(memory_space=pl.ANY)],
        out_specs=pl.BlockSpec(memory_space=pl.ANY),
        out_shape=jax.ShapeDtypeStruct((n_slots, half), U32),
        scratch_shapes=[pltpu.SemaphoreType.DMA(())],
        input_output_aliases={2: 0},
        compiler_params=_cparams(("arbitrary",)),
        name="moe_dispatch",
    )(dest, hfp, slots)


def _expert_kernel(be_ref, nx_ref, nu_ref, xs_ref, wgu_hbm, wdn_hbm, o_ref, wgu_f, wdn_f, wgu_s, wdn_s, run_s, sem,
                   *, layer):
    i = pl.program_id(0)
    e = be_ref[i]

    def fetch(ex, slot):
        return (pltpu.make_async_copy(wgu_hbm.at[layer, ex], wgu_f.at[slot], sem.at[slot, 0]),
                pltpu.make_async_copy(wdn_hbm.at[layer, ex], wdn_f.at[slot], sem.at[slot, 1]))

    @pl.when(i == 0)
    def _():
        run_s[0] = 0
        for cp in fetch(e, 0):
            cp.start()

    changed = jnp.logical_and(i > 0, e != be_ref[jnp.maximum(i - 1, 0)])

    @pl.when(changed)
    def _():
        run_s[0] = run_s[0] + 1

    @pl.when(jnp.logical_or(i == 0, changed))
    def _():
        slot = run_s[0] % 2
        for cp in fetch(e, slot):
            cp.wait()
        nxt = nx_ref[i]

        @pl.when(nxt >= 0)
        def _():
            for cp in fetch(nxt, 1 - slot):
                cp.start()

        wgu_s[...] = wgu_f[slot].astype(BF16)
        wdn_s[...] = wdn_f[slot].astype(BF16)

    @pl.when(i < nu_ref[0])
    def _():
        p = xs_ref[...]
        xa = pltpu.bitcast(p & jnp.uint32(0xFFFF0000), F32).astype(BF16)
        xb = pltpu.bitcast(p << 16, F32).astype(BF16)
        gu = _dot(jnp.concatenate([xa, xb], axis=1), wgu_s[...])
        act = (_silu(gu[:, :D_EXPERT]) * gu[:, D_EXPERT:]).astype(BF16)
        o_ref[...] = _dot(act, wdn_s[...])

    @pl.when(i >= nu_ref[0])
    def _():
        o_ref[...] = jnp.zeros(o_ref.shape, F32)


def _experts(blk_exp, n_used, xs, w_gu, w_dn, layer, tb):
    n_slots = xs.shape[0]
    n_blocks = n_slots // tb
    half = D_MODEL // 2
    ids = jnp.arange(n_blocks, dtype=I32)
    change = jnp.concatenate([jnp.zeros((1,), bool), blk_exp[1:] != blk_exp[:-1]])
    nxt_idx = lax.cummin(jnp.where(change, ids, n_blocks), reverse=True)
    nxt_idx = jnp.concatenate([nxt_idx[1:], jnp.full((1,), n_blocks, I32)])
    nxt_exp = jnp.where(nxt_idx < n_blocks, blk_exp[jnp.minimum(nxt_idx, n_blocks - 1)], -1).astype(I32)
    grid_spec = pltpu.PrefetchScalarGridSpec(
        num_scalar_prefetch=3,
        grid=(n_blocks,),
        in_specs=[pl.BlockSpec((tb, half), lambda i, be, nx, nu: (i, 0)),
                  pl.BlockSpec(memory_space=pl.ANY), pl.BlockSpec(memory_space=pl.ANY)],
        out_specs=pl.BlockSpec((tb, D_MODEL), lambda i, be, nx, nu: (i, 0)),
        scratch_shapes=[pltpu.VMEM((2, D_MODEL, 2 * D_EXPERT), F32), pltpu.VMEM((2, D_EXPERT, D_MODEL), F32),
                        pltpu.VMEM((D_MODEL, 2 * D_EXPERT), BF16), pltpu.VMEM((D_EXPERT, D_MODEL), BF16),
                        pltpu.SMEM((1,), I32), pltpu.SemaphoreType.DMA((2, 2))],
    )
    return pl.pallas_call(
        functools.partial(_expert_kernel, layer=layer),
        grid_spec=grid_spec,
        out_shape=jax.ShapeDtypeStruct((n_slots, D_MODEL), F32),
        compiler_params=_cparams(("arbitrary",)),
        name="moe_experts",
    )(blk_exp, nxt_exp, n_used, xs, w_gu, w_dn)


def _combine_kernel(dest_ref, destn_ref, x_ref, rt_ref, pe_ref, g_ref, wpg_ref, bpg_ref, wple_ref, outs_ref, o_ref,
                    g_s, x_s, hn_s, pe_s, sem):
    i = pl.program_id(0)
    nt = pl.num_programs(0)
    tm = x_ref.shape[0]

    nsl = 8
    sw = D_MODEL // nsl
    per = tm // nsl

    def gather(d_ref, buf, lo, hi):
        for j in range(lo, hi):
            for kk in range(2):
                _row_copy(outs_ref, d_ref[0, 0, kk * tm + j], g_s.at[buf, kk], j, sem.at[buf]).start()

    def drain(buf):
        for _ in range(2 * tm):
            _row_copy(outs_ref, 0, g_s.at[buf, 0], 0, sem.at[buf]).wait()

    @pl.when(i == 0)
    def _():
        gather(dest_ref, 0, 0, tm)

    def step(cur):
        drain(cur)
        rt = rt_ref[...]
        x = x_ref[...] + (g_s[cur, 0] * rt[:, 2:3] + g_s[cur, 1] * rt[:, 3:4])
        x_s[...] = x
        hn_s[...] = _rms(x, g_ref[...]).astype(BF16)
        pe_s[...] = pe_ref[...].astype(BF16)
        for c in range(nsl):
            gather(destn_ref, 1 - cur, c * per, (c + 1) * per)
            cols = slice(c * sw, (c + 1) * sw)
            gate = jax.nn.sigmoid(_dot(hn_s[...], wpg_ref[:, cols]) + bpg_ref[:, cols])
            o_ref[:, cols] = x_s[:, cols] + gate * _dot(pe_s[...], wple_ref[:, cols])

        @pl.when(i == nt - 1)
        def _():
            drain(1 - cur)

    for par in range(2):
        pl.when(i % 2 == par)(functools.partial(step, par))


def _combine(dest, x, rt, pe, outs, lw, tm):
    n = x.shape[0]
    nt = n // tm
    row = lambda w: pl.BlockSpec((tm, w), lambda i: (i, 0))
    return pl.pallas_call(
        _combine_kernel,
        grid=(nt,),
        in_specs=[pl.BlockSpec((1, 1, 2 * tm), lambda i: (i, 0, 0), memory_space=pltpu.SMEM),
                  pl.BlockSpec((1, 1, 2 * tm), lambda i: (jnp.minimum(i + 1, nt - 1), 0, 0), memory_space=pltpu.SMEM),
                  row(D_MODEL), row(LANES), row(D_PLE), _const_spec((1, D_MODEL)),
                  _const_spec((D_MODEL, D_MODEL)), _const_spec((1, D_MODEL)), _const_spec((D_PLE, D_MODEL)),
                  pl.BlockSpec(memory_space=pl.ANY)],
        out_specs=row(D_MODEL),
        out_shape=jax.ShapeDtypeStruct((n, D_MODEL), F32),
        scratch_shapes=[pltpu.VMEM((2, 2, tm, D_MODEL), F32), pltpu.VMEM((tm, D_MODEL), F32),
                        pltpu.VMEM((tm, D_MODEL), BF16), pltpu.VMEM((tm, D_PLE), BF16), pltpu.SemaphoreType.DMA((2,))],
        compiler_params=_cparams(("arbitrary",)),
        name="moe_combine_ple",
    )(dest, dest, x, rt, pe, lw["g_ple"], lw["w_pg"], lw["b_pg"], lw["w_ple"], outs)


def _tile_dest(dest, tm):
    n = dest.shape[1]
    return dest.reshape(2, n // tm, tm).transpose(1, 0, 2).reshape(n // tm, 1, 2 * tm)


def _moe_ple(x, hfp, rt, rtt, pe, lw, slots):
    n = x.shape[0]
    tb = min(256, n)
    rank, cnt = _rank(rtt)
    counts = cnt[:, 0].astype(I32)
    padded = (counts + tb - 1) // tb * tb
    pend = jnp.cumsum(padded)
    pstart = pend - padded
    e = rtt[0:2].astype(I32)
    eids = jnp.arange(N_EXPERTS, dtype=I32)
    start_e = jnp.sum(jnp.where(e[..., None] == eids, pstart, 0), axis=-1)
    dest = start_e + rank[0:2].astype(I32)
    n_blocks = -(-2 * n // tb) + N_EXPERTS
    blk_start = jnp.arange(n_blocks, dtype=I32) * tb
    blk_exp = jnp.minimum(jnp.sum((pend[None, :] <= blk_start[:, None]).astype(I32), axis=1), N_EXPERTS - 1)
    n_used = (pend[-1:] // tb).astype(I32)
    td = min(512, n)
    if slots is None:
        slots = jnp.zeros((n_blocks * tb, D_MODEL // 2), U32)
    xs = _dispatch(_tile_dest(dest, td), hfp, slots, td)
    outs = _experts(blk_exp, n_used, xs, lw["w_gate_up"], lw["w_down"], lw["layer"], tb)
    tc = min(256, n)
    return _combine(_tile_dest(dest, tc), x, rt, pe, outs, lw, tc), xs


def _layer(x, pe, lw, cos, sin, lam_init, slots=None):
    b, seq, _ = x.shape
    n = b * seq
    pool_u, q, k, v, z, xbc, dt = _inproj(x.reshape(n, D_MODEL), lw, cos, sin, seq)
    sh = lambda a: a.reshape(b, seq, a.shape[-1])
    y_pool = _pool(sh(pool_u), lw)
    y_da = _attention(sh(q), sh(k), sh(v), lw["lam"], lw["subln_g"], lam_init)
    y_ssd = _ssd(sh(xbc), sh(dt), sh(z), lw)
    x2, hfp, rt, rtt = _outproj(x.reshape(n, D_MODEL), y_pool.reshape(n, D_POOL), y_da.reshape(n, D_DA),
                                y_ssd.reshape(n, D_SSD), lw)
    x3, slots = _moe_ple(x2, hfp, rt, rtt, pe.reshape(n, D_PLE), lw, slots)
    return x3.reshape(b, seq, D_MODEL), slots


def _rope_tables(seq):
    half = DA_DH // 2
    inv = jnp.power(ROPE_THETA, -jnp.arange(half, dtype=F32) * 2.0 / DA_DH)
    ang = jnp.arange(seq, dtype=F32)[:, None] * inv[None, :]
    cos = jnp.cos(ang)
    sin = jnp.sin(ang)
    return jnp.tile(cos, (1, 4)), jnp.tile(jnp.concatenate([-sin, sin], axis=1), (1, 2))


def _prep_layer(i, w):
    row = lambda a: a.reshape(1, -1).astype(F32)
    w_in = w["w_in"][i].astype(BF16)
    o = 0
    lw = {"g_mix": row(w["norm_mix_g"][i])}
    for name, width in (("w_pool", D_POOL), ("w_q", D_DA), ("w_k", D_DA), ("w_v", D_DA), ("w_z", D_SSD), ("w_xbc", D_XBC)):
        lw[name] = w_in[:, o:o + width]
        o += width
    lw["w_dt"] = jnp.pad(w_in[:, o:], ((0, 0), (0, LANES - 2 * SSD_HEADS)))
    lw["gq"] = row(jnp.tile(w["q_norm_g"][i], 2 * DA_HEADS)) * (DA_DH ** -0.5 * math.log2(math.e))
    lw["gk"] = row(jnp.tile(w["k_norm_g"][i], 2 * DA_HEADS))
    lw["pool_w"] = w["pool_w"][i].astype(BF16)
    lw["pool_scale"] = row(w["pool_scale"][i])
    lf = w["lam_qk"][i].astype(F32)
    lam_init = 0.8 - 0.6 * math.exp(-0.3 * i)
    lam = jnp.exp(jnp.sum(lf[0] * lf[1])) - jnp.exp(jnp.sum(lf[2] * lf[3])) + lam_init
    lw["lam"] = jnp.full((1, DA_DV), lam, F32)
    lw["subln_g"] = row(w["subln_g"][i])
    lw["conv_w"] = jnp.pad(w["conv_w"][i].astype(F32), ((0, 8 - CONV_K), (0, 0)))
    lw["conv_b"] = row(w["conv_b"][i])
    pad_row = lambda a: jnp.pad(a.reshape(1, -1).astype(F32), ((0, 0), (0, LANES - 2 * SSD_HEADS)))
    lw["dt_bias"] = pad_row(w["dt_bias"][i])
    lw["a_log"] = pad_row(w["a_log"][i])
    lw["d_skip"] = row(jnp.repeat(w["d_skip"][i], SSD_HEAD_DIM))
    lw["ssd_norm_g"] = row(w["ssd_norm_g"][i])
    w_out = w["w_out"][i].astype(BF16)
    lw["wo_pool"] = w_out[:D_POOL]
    lw["wo_da"] = w_out[D_POOL:D_POOL + D_DA]
    lw["wo_ssd"] = w_out[D_POOL + D_DA:]
    lw["g_ffn"] = row(w["norm_ffn_g"][i])
    wr = jnp.zeros((D_MODEL, LANES), F32)
    wr = wr.at[:, :N_GROUPS].set(w["router_coarse_w"][i]).at[:, 8:8 + N_EXPERTS].set(w["router_fine_w"][i])
    lw["wr_hi"] = wr.astype(BF16)
    lw["wr_lo"] = (wr - lw["wr_hi"].astype(F32)).astype(BF16)
    rb = jnp.zeros((1, LANES), F32)
    lw["r_bias"] = rb.at[0, :N_GROUPS].set(w["router_coarse_b"][i]).at[0, 8:8 + N_EXPERTS].set(w["router_fine_b"][i])
    lw["w_gate_up"] = w["w_gate_up"]
    lw["w_down"] = w["w_down"]
    lw["layer"] = i
    lw["g_ple"] = row(w["norm_ple_g"][i])
    lw["w_pg"] = w["w_ple_gate"][i].astype(BF16)
    lw["b_pg"] = row(w["b_ple_gate"][i])
    lw["w_ple"] = w["w_ple"][i].astype(BF16)
    return lw, lam_init


def kernel(x_prompt, x_sample, p_prompt, p_sample, w_in, w_out, pool_w, pool_scale, q_norm_g, k_norm_g, lam_qk, subln_g, conv_w, conv_b, a_log, dt_bias, d_skip, ssd_norm_g, norm_mix_g, norm_ffn_g, router_coarse_w, router_coarse_b, router_fine_w, router_fine_b, w_gate_up, w_down, norm_ple_g, w_ple, w_ple_gate, b_ple_gate):
    w = dict(w_in=w_in, w_out=w_out, pool_w=pool_w, pool_scale=pool_scale, q_norm_g=q_norm_g, k_norm_g=k_norm_g,
             lam_qk=lam_qk, subln_g=subln_g, conv_w=conv_w, conv_b=conv_b, a_log=a_log, dt_bias=dt_bias,
             d_skip=d_skip, ssd_norm_g=ssd_norm_g, norm_mix_g=norm_mix_g, norm_ffn_g=norm_ffn_g,
             router_coarse_w=router_coarse_w, router_coarse_b=router_coarse_b, router_fine_w=router_fine_w,
             router_fine_b=router_fine_b, w_gate_up=w_gate_up, w_down=w_down, norm_ple_g=norm_ple_g,
             w_ple=w_ple, w_ple_gate=w_ple_gate, b_ple_gate=b_ple_gate)
    depth = w_in.shape[0]
    xs = [x_prompt, x_sample]
    ps = [p_prompt, p_sample]
    tables = [_rope_tables(x.shape[1]) for x in xs]
    slots = [None, None]
    for i in range(depth):
        lw, lam_init = _prep_layer(i, w)
        res = [_layer(x, p[i], lw, cs[0], cs[1], lam_init, sl) for x, p, cs, sl in zip(xs, ps, tables, slots)]
        xs = [r[0] for r in res]
        slots = [r[1] for r in res]
    return tuple(xs)
```

```python
import functools
import math

import jax
import jax.numpy as jnp
from jax import lax
from jax.experimental import pallas as pl
from jax.experimental.pallas import tpu as pltpu

F32 = jnp.float32
BF16 = jnp.bfloat16
I32 = jnp.int32
U32 = jnp.uint32

EPS = 1e-6
D_MODEL = 2048
D_PLE = 256
POOL_WINDOWS = (2, 4, 8, 16)
POOL_CH = 128
D_POOL = 512
DA_HEADS = 4
DA_DH = 64
DA_DV = 128
D_DA = 512
ROPE_THETA = 10000.0
SSD_HEADS = 16
SSD_HEAD_DIM = 64
D_SSD = 1024
SSD_GROUPS = 2
SSD_STATE = 128
CONV_K = 5
CHUNK = 128
D_XBC = 1536
N_GROUPS = 4
EXP_PER_GROUP = 8
N_EXPERTS = 32
D_EXPERT = 512

LANES = 128
HALO = 16
VMEM_LIMIT = 56 * 2**20


def _cparams(sem):
    return pltpu.CompilerParams(dimension_semantics=sem, vmem_limit_bytes=VMEM_LIMIT)


def _const_spec(shape):
    nd = len(shape)
    return pl.BlockSpec(shape, lambda *_: (0,) * nd, pipeline_mode=pl.Buffered(1))


def _dot(a, b):
    return jnp.dot(a, b, preferred_element_type=F32)


def _dot_nt(a, b):
    return lax.dot_general(a, b, (((1,), (1,)), ((), ())), preferred_element_type=F32)


def _split2(x):
    hi = x.astype(BF16)
    lo = (x - hi.astype(F32)).astype(BF16)
    return hi, lo


def _split3(x):
    a1 = x.astype(BF16)
    r1 = x - a1.astype(F32)
    a2 = r1.astype(BF16)
    a3 = (r1 - a2.astype(F32)).astype(BF16)
    return a1, a2, a3


def _silu(x):
    return x * jax.nn.sigmoid(x)


def _rms(x, g):
    ms = jnp.mean(x * x, axis=-1, keepdims=True)
    return x * lax.rsqrt(ms + EPS) * g


def _qk_prep(x, g, cos, sin, bd, first_half):
    outs = []
    for c in range(D_DA // LANES):
        xc = x[:, c * LANES:(c + 1) * LANES]
        hi, lo = _split2(xc * xc)
        ss = _dot(hi, bd) + _dot(lo, bd)
        xn = xc * lax.rsqrt(ss * (1.0 / DA_DH) + EPS) * g[:, c * LANES:(c + 1) * LANES]
        sw = jnp.where(first_half, pltpu.roll(xn, LANES - 32, 1), pltpu.roll(xn, 32, 1))
        outs.append((xn * cos + sw * sin).astype(BF16))
    return jnp.concatenate(outs, axis=1)


def _inproj_kernel(x_ref, g_ref, wp_ref, wq_ref, wk_ref, wv_ref, wz_ref, wx_ref, wd_ref,
                   gq_ref, gk_ref, cos_ref, sin_ref,
                   pool_o, q_o, k_o, v_o, z_o, xbc_o, dt_o):
    h = _rms(x_ref[...], g_ref[...]).astype(BF16)
    pool_o[...] = _dot(h, wp_ref[...])
    v_o[...] = _dot(h, wv_ref[...]).astype(BF16)
    z_o[...] = _dot(h, wz_ref[...]).astype(BF16)
    xbc_o[...] = _dot(h, wx_ref[...]).astype(BF16)
    dt_o[...] = _dot(h, wd_ref[...])
    r = lax.broadcasted_iota(I32, (LANES, LANES), 0) // DA_DH
    c = lax.broadcasted_iota(I32, (LANES, LANES), 1) // DA_DH
    bd = jnp.where(r == c, 1.0, 0.0).astype(BF16)
    first_half = (lax.broadcasted_iota(I32, (1, LANES), 1) % DA_DH) < (DA_DH // 2)
    cos = cos_ref[...]
    sin = sin_ref[...]
    q_o[...] = _qk_prep(_dot(h, wq_ref[...]), gq_ref[...], cos, sin, bd, first_half)
    k_o[...] = _qk_prep(_dot(h, wk_ref[...]), gk_ref[...], cos, sin, bd, first_half)


def _inproj(x, lw, cos, sin, seq):
    n = x.shape[0]
    tm = min(512, seq)
    nseq = seq // tm
    row = lambda w: pl.BlockSpec((tm, w), lambda i: (i, 0))
    pos = pl.BlockSpec((tm, LANES), lambda i: (i % nseq, 0))
    ws = [lw["w_pool"], lw["w_q"], lw["w_k"], lw["w_v"], lw["w_z"], lw["w_xbc"], lw["w_dt"]]
    return pl.pallas_call(
        _inproj_kernel,
        grid=(n // tm,),
        in_specs=[row(D_MODEL), _const_spec((1, D_MODEL))] + [_const_spec(w.shape) for w in ws]
        + [_const_spec((1, D_DA)), _const_spec((1, D_DA)), pos, pos],
        out_specs=[row(D_POOL), row(D_DA), row(D_DA), row(D_DA), row(D_SSD), row(D_XBC), row(LANES)],
        out_shape=[jax.ShapeDtypeStruct((n, D_POOL), F32), jax.ShapeDtypeStruct((n, D_DA), BF16),
                   jax.ShapeDtypeStruct((n, D_DA), BF16), jax.ShapeDtypeStruct((n, D_DA), BF16),
                   jax.ShapeDtypeStruct((n, D_SSD), BF16), jax.ShapeDtypeStruct((n, D_XBC), BF16),
                   jax.ShapeDtypeStruct((n, LANES), F32)],
        compiler_params=_cparams(("arbitrary",)),
        name="inproj",
    )(x, lw["g_mix"], *ws, lw["gq"], lw["gk"], cos, sin)


def _pool_kernel(prev_ref, cur_ref, next_ref, w_ref, sc_ref, o_ref, *, seq, tp):
    i = pl.program_id(1)
    nt = pl.num_programs(1)
    cur = cur_ref[0]
    prev = jnp.where(i > 0, prev_ref[0], 0.0)
    nxt = jnp.where(i < nt - 1, next_ref[0], 0.0)
    ext = jnp.concatenate([prev, cur, nxt], axis=0)
    n = tp + 2 * HALO
    t = i * tp + lax.broadcasted_iota(I32, (tp, 1), 0)

    def shifted(a, d):
        return pltpu.roll(a, (-d) % n, 0)

    outs = []
    for gi, win in enumerate(POOL_WINDOWS):
        xg = ext[:, gi * POOL_CH:(gi + 1) * POOL_CH]
        s = xg + shifted(xg, -1)
        half = 1
        while 2 * half < win:
            s = shifted(s, -half) + shifted(s, half)
            half *= 2
        left = win // 2
        right = win - 1 - left
        cnt = (jnp.minimum(t + right + 1, seq) - jnp.maximum(t - left, 0)).astype(F32)
        dlt = s[HALO:HALO + tp] / cnt - cur[:, gi * POOL_CH:(gi + 1) * POOL_CH]
        outs.append(_dot(dlt.astype(BF16), w_ref[gi]))
    o_ref[0] = (jnp.concatenate(outs, axis=1) * sc_ref[...]).astype(BF16)


def _pool(u, lw):
    b, seq, _ = u.shape
    tp = min(512, seq)
    r = tp // HALO
    nh = seq // HALO
    return pl.pallas_call(
        functools.partial(_pool_kernel, seq=seq, tp=tp),
        grid=(b, seq // tp),
        in_specs=[pl.BlockSpec((1, HALO, D_POOL), lambda bi, i: (bi, jnp.maximum(i * r - 1, 0), 0)),
                  pl.BlockSpec((1, tp, D_POOL), lambda bi, i: (bi, i, 0)),
                  pl.BlockSpec((1, HALO, D_POOL), lambda bi, i: (bi, jnp.minimum((i + 1) * r, nh - 1), 0)),
                  _const_spec((4, POOL_CH, POOL_CH)), _const_spec((1, D_POOL))],
        out_specs=pl.BlockSpec((1, tp, D_POOL), lambda bi, i: (bi, i, 0)),
        out_shape=jax.ShapeDtypeStruct((b, seq, D_POOL), BF16),
        compiler_params=_cparams(("arbitrary", "arbitrary")),
        name="pool",
    )(u, u, u, lw["pool_w"], lw["pool_scale"])


def _attn_kernel(q_ref, k_ref, v_ref, lam_ref, g_ref, o_ref, q_s, vt_s, s_s, p_s, m_s, al_s, acc_s, *, post_scale, tk):
    nk = k_ref.shape[1] // tk
    vrows = vt_s.shape[1]

    @pl.when(pl.program_id(2) == 0)
    def _():
        tail = jnp.where(lax.broadcasted_iota(I32, (vrows - DA_DV, tk), 0) == 0, 1.0, 0.0).astype(BF16)

        def tr(j, carry):
            vt_s[j, 0:DA_DV, :] = v_ref[0, pl.ds(pl.multiple_of(j * tk, tk), tk), :].astype(F32).T.astype(BF16)
            vt_s[j, DA_DV:, :] = tail
            return carry

        lax.fori_loop(0, nk, tr, 0)

    q = q_ref[0]
    lane = lax.broadcasted_iota(I32, q.shape, 1)
    zero = jnp.zeros_like(q)
    q_s[0] = jnp.where(lane < DA_DH, q, zero)
    q_s[1] = jnp.where(lane >= DA_DH, q, zero)
    m_s[...] = jnp.full(m_s.shape, -jnp.inf, F32)
    acc_s[...] = jnp.zeros(acc_s.shape, F32)

    def scores(j, slot):
        k = k_ref[0, pl.ds(pl.multiple_of(j * tk, tk), tk), :]
        for c in range(2):
            s_s[slot, c] = _dot_nt(k, q_s[c])

    strip = 64

    def softmax(slot):
        for c in range(2):
            m_prev = m_s[c]
            m_new = jnp.maximum(m_prev, jnp.max(s_s[slot, c], axis=0, keepdims=True))
            al_s[slot, c] = jnp.exp2(m_prev - m_new)
            m_s[c] = m_new
            for r in range(0, tk, strip):
                p_s[slot, c, r:r + strip, :] = jnp.exp2(s_s[slot, c, r:r + strip, :] - m_new).astype(BF16)

    def pv(j, slot):
        vt = vt_s[j]
        for c in range(2):
            acc_s[c] = al_s[slot, c] * acc_s[c] + _dot(vt, p_s[slot, c])

    scores(0, 0)
    if nk > 1:
        scores(1, 1)
    softmax(0)

    def pair(t, carry):
        for par in range(2):
            j = 2 * t + 1 + par
            slot = 1 - par
            scores(j + 1, 1 - slot)
            pv(j - 1, 1 - slot)
            softmax(slot)
        return carry

    if nk > 2:
        lax.fori_loop(0, (nk - 2) // 2, pair, 0)
    if nk > 1:
        pv(nk - 2, 0)
        softmax(1)
        pv(nk - 1, 1)
    else:
        pv(0, 0)
    a0 = acc_s[0]
    a1 = acc_s[1]
    ot = a0[:DA_DV] / a0[DA_DV:DA_DV + 1] - lam_ref[0:1, 0:1] * (a1[:DA_DV] / a1[DA_DV:DA_DV + 1])
    o_ref[0] = (_rms(ot.T, g_ref[...]) * post_scale).astype(BF16)


def _attention(q, k, v, lam, subln_g, lam_init):
    b, seq, _ = q.shape
    tq = min(1024, seq)
    tk = min(512, seq)
    nk = seq // tk
    assert nk == 1 or nk % 2 == 0
    vrows = DA_DV + 16
    qspec = pl.BlockSpec((1, tq, DA_DV), lambda bi, h, qi: (bi, qi, h))
    kspec = pl.BlockSpec((1, seq, DA_DV), lambda bi, h, qi: (bi, 0, h))
    return pl.pallas_call(
        functools.partial(_attn_kernel, post_scale=1.0 - lam_init, tk=tk),
        grid=(b, DA_HEADS, seq // tq),
        in_specs=[qspec, kspec, kspec, _const_spec((1, DA_DV)), _const_spec((1, DA_DV))],
        out_specs=qspec,
        out_shape=jax.ShapeDtypeStruct((b, seq, D_DA), BF16),
        scratch_shapes=[pltpu.VMEM((2, tq, DA_DV), BF16), pltpu.VMEM((nk, vrows, tk), BF16),
                        pltpu.VMEM((2, 2, tk, tq), F32), pltpu.VMEM((2, 2, tk, tq), BF16),
                        pltpu.VMEM((2, 1, tq), F32), pltpu.VMEM((2, 2, 1, tq), F32),
                        pltpu.VMEM((2, vrows, tq), F32)],
        compiler_params=_cparams(("arbitrary", "arbitrary", "arbitrary")),
        name="diff_attn",
    )(q, k, v, lam, subln_g)


def _ssd_kernel(*refs, rev):
    if rev:
        (prev_ref, cur_ref, next_ref, dt_ref, cw_ref, cb_ref, dtb_ref, alog_ref,
         yf_ref, z_ref, dsk_ref, gn_ref, o_ref, st_s) = refs
    else:
        (prev_ref, cur_ref, next_ref, dt_ref, cw_ref, cb_ref, dtb_ref, alog_ref, o_ref, st_s) = refs
    c = pl.program_id(1)
    nc = pl.num_programs(1)
    cc = nc - 1 - c if rev else c
    d = 1 if rev else 0

    @pl.when(c == 0)
    def _():
        st_s[...] = jnp.zeros(st_s.shape, F32)

    zero_h = jnp.zeros((HALO, D_XBC), BF16)
    srow = lax.broadcasted_iota(I32, (CHUNK, 2 * CHUNK), 0)
    scol = lax.broadcasted_iota(I32, (CHUNK, 2 * CHUNK), 1)
    gn = SSD_GROUPS * SSD_STATE
    ri = lax.broadcasted_iota(I32, (CHUNK, CHUNK), 0)
    ci = lax.broadcasted_iota(I32, (CHUNK, CHUNK), 1)
    mask = (ci >= ri) if rev else (ci <= ri)
    tri = jnp.where(mask, 1.0, 0.0).astype(BF16)
    last = 0 if rev else CHUNK - 1
    er = lax.broadcasted_iota(I32, (LANES, D_SSD), 0)
    ec = lax.broadcasted_iota(I32, (LANES, D_SSD), 1)
    expand = jnp.where(er == d * SSD_HEADS + ec // SSD_HEAD_DIM, 1.0, 0.0).astype(BF16)
    hg = SSD_HEADS // SSD_GROUPS
    gw = hg * SSD_HEAD_DIM
    lane = lax.broadcasted_iota(I32, (CHUNK, LANES), 1)

    def widen(vv):
        hi, lo = _split2(vv)
        return _dot(hi, expand) + _dot(lo, expand)

    def one(bb):
        cur = cur_ref[bb]
        prev = jnp.where(cc > 0, prev_ref[bb], zero_h)
        nxt = jnp.where(cc < nc - 1, next_ref[bb], zero_h)
        ext = jnp.concatenate([prev, cur, nxt, jnp.zeros((2 * CHUNK - CHUNK - 2 * HALO, D_XBC), BF16)], axis=0)
        acc = jnp.broadcast_to(cb_ref[...], (CHUNK, D_XBC))
        for kk in range(CONV_K):
            sh = kk - CONV_K // 2
            if sh == 0:
                xk = cur.astype(F32)
            else:
                xk = _dot(jnp.where(scol == srow + (HALO + sh), 1.0, 0.0).astype(BF16), ext)
            acc = acc + xk * cw_ref[kk:kk + 1, :]
        xc = _silu(acc)
        xs = xc[:, :D_SSD]
        bm = xc[:, D_SSD:D_SSD + gn]
        cm = xc[:, D_SSD + gn:]

        x = dt_ref[bb] + dtb_ref[...]
        dtv = jnp.maximum(x, 0.0) + jnp.log1p(jnp.exp(-jnp.abs(x)))
        a = dtv * (-jnp.exp(alog_ref[...]))
        a1, a2, a3 = _split3(a)
        cs = _dot(tri, a1) + _dot(tri, a2) + _dot(tri, a3)
        cs_t = cs.T
        cs_end = cs[last:last + 1, :]
        e_dec = jnp.exp(cs)
        w_dec = dtv * jnp.exp(cs_end - cs)
        xdt = (xs * widen(dtv)).astype(BF16)
        xw = (xs * widen(w_dec)).astype(BF16)
        e_wide = widen(e_dec)

        ys = []
        for g in range(SSD_GROUPS):
            bg = bm[:, g * SSD_STATE:(g + 1) * SSD_STATE]
            cg = cm[:, g * SSD_STATE:(g + 1) * SSD_STATE].astype(BF16)
            cb = _dot_nt(cg, bg.astype(BF16))
            st = st_s[bb, g]
            y_off = _dot(cg, st.astype(BF16)) * e_wide[:, g * gw:(g + 1) * gw]
            st_s[bb, g] = (st * e_wide[last:last + 1, g * gw:(g + 1) * gw]
                           + _dot(bg.T.astype(BF16), xw[:, g * gw:(g + 1) * gw]))
            for pr in range(hg // 2):
                ms_ = []
                for hh in range(2):
                    col = d * SSD_HEADS + g * hg + pr * 2 + hh
                    diff = cs[:, col:col + 1] - cs_t[col:col + 1, :]
                    ms_.append((cb * jnp.exp(jnp.where(mask, diff, -1e30))).astype(BF16))
                q0 = g * hg // 2 + pr
                xp = xdt[:, q0 * LANES:(q0 + 1) * LANES]
                zero = jnp.zeros_like(xp)
                rhs = jnp.concatenate([jnp.where(lane < SSD_HEAD_DIM, xp, zero),
                                       jnp.where(lane >= SSD_HEAD_DIM, xp, zero)], axis=0)
                yd = _dot(jnp.concatenate(ms_, axis=1), rhs)
                ys.append(yd + y_off[:, pr * LANES:(pr + 1) * LANES])
        y = jnp.concatenate(ys, axis=1)

        if rev:
            y = yf_ref[bb] + y + xs * dsk_ref[...]
            y = y * _silu(z_ref[bb].astype(F32))
            o_ref[bb] = _rms(y, gn_ref[...]).astype(BF16)
        else:
            o_ref[bb] = y

    for bb in range(cur_ref.shape[0]):
        one(bb)


def _ssd(xbc, dt, z, lw):
    b, seq, _ = xbc.shape
    nc = seq // CHUNK
    r = CHUNK // HALO
    nh = seq // HALO
    nb = 2 if b % 2 == 0 else 1

    def specs(rev):
        ch = (lambda c: nc - 1 - c) if rev else (lambda c: c)
        return [pl.BlockSpec((nb, HALO, D_XBC), lambda bi, c: (bi, jnp.maximum(ch(c) * r - 1, 0), 0)),
                pl.BlockSpec((nb, CHUNK, D_XBC), lambda bi, c: (bi, ch(c), 0)),
                pl.BlockSpec((nb, HALO, D_XBC), lambda bi, c: (bi, jnp.minimum((ch(c) + 1) * r, nh - 1), 0)),
                pl.BlockSpec((nb, CHUNK, LANES), lambda bi, c: (bi, ch(c), 0)),
                _const_spec((8, D_XBC)), _const_spec((1, D_XBC)), _const_spec((1, LANES)), _const_spec((1, LANES))]

    scratch = [pltpu.VMEM((nb, SSD_GROUPS, SSD_STATE, D_SSD // SSD_GROUPS), F32)]
    common = (xbc, xbc, xbc, dt, lw["conv_w"], lw["conv_b"], lw["dt_bias"], lw["a_log"])
    y_f = pl.pallas_call(
        functools.partial(_ssd_kernel, rev=False),
        grid=(b // nb, nc),
        in_specs=specs(False),
        out_specs=pl.BlockSpec((nb, CHUNK, D_SSD), lambda bi, c: (bi, c, 0)),
        out_shape=jax.ShapeDtypeStruct((b, seq, D_SSD), F32),
        scratch_shapes=scratch,
        compiler_params=_cparams(("arbitrary", "arbitrary")),
        name="ssd_fwd",
    )(*common)
    rspec = pl.BlockSpec((nb, CHUNK, D_SSD), lambda bi, c: (bi, nc - 1 - c, 0))
    return pl.pallas_call(
        functools.partial(_ssd_kernel, rev=True),
        grid=(b // nb, nc),
        in_specs=specs(True) + [rspec, rspec, _const_spec((1, D_SSD)), _const_spec((1, D_SSD))],
        out_specs=rspec,
        out_shape=jax.ShapeDtypeStruct((b, seq, D_SSD), BF16),
        scratch_shapes=scratch,
        compiler_params=_cparams(("arbitrary", "arbitrary")),
        name="ssd_bwd",
    )(*common, y_f, z, lw["d_skip"], lw["ssd_norm_g"])


def _first_argmax(v, rowid, big):
    vmax = jnp.max(v, axis=0, keepdims=True)
    idx = jnp.min(jnp.where(v == vmax, rowid, big), axis=0, keepdims=True)
    return vmax, idx


def _outproj_kernel(x_ref, yp_ref, ya_ref, ys_ref, wp_ref, wa_ref, ws_ref, g_ref, wrh_ref, wrl_ref, rb_ref,
                    xo_ref, hf_ref, rt_ref, rtt_ref):
    x = x_ref[...] + (_dot(yp_ref[...], wp_ref[...]) + _dot(ya_ref[...], wa_ref[...]) + _dot(ys_ref[...], ws_ref[...]))
    xo_ref[...] = x
    hi, lo = _split2(_rms(x, g_ref[...]))
    hb = pltpu.bitcast(hi.astype(F32), U32)
    half = D_MODEL // 2
    hf_ref[...] = hb[:, :half] | (hb[:, half:] >> 16)
    wrh = wrh_ref[...]
    lg = _dot(hi, wrh) + _dot(lo, wrh) + _dot(hi, wrl_ref[...]) + rb_ref[...]
    lgt = lg.T
    tm = lgt.shape[1]
    rowid = lax.broadcasted_iota(I32, (8, tm), 0)
    coarse = jnp.where(rowid < N_GROUPS, lgt[0:8], -jnp.inf)
    cmax = jnp.max(coarse, axis=0, keepdims=True)
    ce = jnp.exp(coarse - cmax)
    _, grp = _first_argmax(ce, rowid, 99)
    gp = 1.0 / jnp.sum(ce, axis=0, keepdims=True)
    fine = jnp.zeros((8, tm), F32)
    for g in range(N_GROUPS):
        fine = jnp.where(grp == g, lgt[8 + 8 * g:16 + 8 * g], fine)
    fe = jnp.exp(fine - jnp.max(fine, axis=0, keepdims=True))
    fp = fe / jnp.sum(fe, axis=0, keepdims=True)
    v0, i0 = _first_argmax(fp, rowid, 99)
    v1, i1 = _first_argmax(jnp.where(rowid == i0, -1.0, fp), rowid, 99)
    den = v0 + v1
    vals = [(grp * EXP_PER_GROUP + i0).astype(F32), (grp * EXP_PER_GROUP + i1).astype(F32), gp * v0 / den, gp * v1 / den]
    rtt = jnp.zeros((8, tm), F32)
    for j, vv in enumerate(vals):
        rtt = jnp.where(rowid == j, vv, rtt)
    rtt_ref[...] = rtt
    rt_ref[...] = jnp.concatenate([rtt, jnp.zeros((LANES - 8, tm), F32)], axis=0).T


def _outproj(x, y_pool, y_da, y_ssd, lw):
    n = x.shape[0]
    tm = min(512, n)
    row = lambda w: pl.BlockSpec((tm, w), lambda i: (i, 0))
    ws = [lw["wo_pool"], lw["wo_da"], lw["wo_ssd"]]
    return pl.pallas_call(
        _outproj_kernel,
        grid=(n // tm,),
        in_specs=[row(D_MODEL), row(D_POOL), row(D_DA), row(D_SSD)] + [_const_spec(w.shape) for w in ws]
        + [_const_spec((1, D_MODEL)), _const_spec((D_MODEL, LANES)), _const_spec((D_MODEL, LANES)), _const_spec((1, LANES))],
        out_specs=[row(D_MODEL), row(D_MODEL // 2), row(LANES), pl.BlockSpec((8, tm), lambda i: (0, i))],
        out_shape=[jax.ShapeDtypeStruct((n, D_MODEL), F32), jax.ShapeDtypeStruct((n, D_MODEL // 2), U32),
                   jax.ShapeDtypeStruct((n, LANES), F32), jax.ShapeDtypeStruct((8, n), F32)],
        compiler_params=_cparams(("arbitrary",)),
        name="outproj_router",
    )(x, y_pool, y_da, y_ssd, *ws, lw["g_ffn"], lw["wr_hi"], lw["wr_lo"], lw["r_bias"])


def _rank_kernel(rtt_ref, rank_ref, cnt_ref):
    @pl.when(pl.program_id(0) == 0)
    def _():
        cnt_ref[...] = jnp.zeros(cnt_ref.shape, F32)

    t = rtt_ref.shape[1]
    eid = lax.broadcasted_iota(I32, (N_EXPERTS, t), 0)
    oh0 = jnp.where(eid == rtt_ref[0:1, :].astype(I32), 1.0, 0.0)
    oh1 = jnp.where(eid == rtt_ref[1:2, :].astype(I32), 1.0, 0.0)
    oh = oh0 + oh1
    before = jnp.where(lax.broadcasted_iota(I32, (t, t), 0) < lax.broadcasted_iota(I32, (t, t), 1), 1.0, 0.0)
    pre = _dot(oh.astype(BF16), before.astype(BF16)) + cnt_ref[:, 0:1]
    r0 = jnp.sum(oh0 * pre, axis=0, keepdims=True)
    r1 = jnp.sum(oh1 * pre, axis=0, keepdims=True)
    rowid = lax.broadcasted_iota(I32, (8, t), 0)
    rank_ref[...] = jnp.where(rowid == 0, r0, jnp.where(rowid == 1, r1, 0.0))
    cnt_ref[...] = cnt_ref[...] + jnp.sum(oh, axis=1, keepdims=True)


def _rank(rtt):
    n = rtt.shape[1]
    t = min(512, n)
    return pl.pallas_call(
        _rank_kernel,
        grid=(n // t,),
        in_specs=[pl.BlockSpec((8, t), lambda i: (0, i))],
        out_specs=[pl.BlockSpec((8, t), lambda i: (0, i)), pl.BlockSpec((N_EXPERTS, LANES), lambda i: (0, 0))],
        out_shape=[jax.ShapeDtypeStruct((8, n), F32), jax.ShapeDtypeStruct((N_EXPERTS, LANES), F32)],
        compiler_params=_cparams(("arbitrary",)),
        name="moe_rank",
    )(rtt)


def _row_copy(src, i, dst, j, sem):
    return pltpu.make_async_copy(src.at[pl.ds(i, 1)], dst.at[pl.ds(j, 1)], sem)


def _dispatch_kernel(dest_ref, hf_ref, xs_in_ref, xs_ref, sem):
    del xs_in_ref
    tm = hf_ref.shape[0]

    for j in range(tm):
        for kk in range(2):
            _row_copy(hf_ref, j, xs_ref, dest_ref[0, 0, kk * tm + j], sem).start(priority=kk)
    for _ in range(2 * tm):
        _row_copy(hf_ref, 0, xs_ref, 0, sem).wait()


def _dispatch(dest, hfp, slots, tm):
    n_slots = slots.shape[0]
    n = hfp.shape[0]
    half = D_MODEL // 2
    return pl.pallas_call(
        _dispatch_kernel,
        grid=(n // tm,),
        in_specs=[pl.BlockSpec((1, 1, 2 * tm), lambda i: (i, 0, 0), memory_space=pltpu.SMEM),
                  pl.BlockSpec((tm, half), lambda i: (i, 0)),
                  pl.BlockSpec(memory_space=pl.ANY)],
        out_specs=pl.BlockSpec(memory_space=pl.ANY),
        out_shape=jax.ShapeDtypeStruct((n_slots, half), U32),
        scratch_shapes=[pltpu.SemaphoreType.DMA(())],
        input_output_aliases={2: 0},
        compiler_params=_cparams(("arbitrary",)),
        name="moe_dispatch",
    )(dest, hfp, slots)


def _expert_kernel(be_ref, nx_ref, nu_ref, xs_ref, wgu_hbm, wdn_hbm, o_ref, wgu_f, wdn_f, wgu_s, wdn_s, run_s, sem,
                   *, layer):
    i = pl.program_id(0)
    e = be_ref[i]

    def fetch(ex, slot):
        return (pltpu.make_async_copy(wgu_hbm.at[layer, ex], wgu_f.at[slot], sem.at[slot, 0]),
                pltpu.make_async_copy(wdn_hbm.at[layer, ex], wdn_f.at[slot], sem.at[slot, 1]))

    @pl.when(i == 0)
    def _():
        run_s[0] = 0
        for cp in fetch(e, 0):
            cp.start()

    changed = jnp.logical_and(i > 0, e != be_ref[jnp.maximum(i - 1, 0)])

    @pl.when(changed)
    def _():
        run_s[0] = run_s[0] + 1

    @pl.when(jnp.logical_or(i == 0, changed))
    def _():
        slot = run_s[0] % 2
        for cp in fetch(e, slot):
            cp.wait()
        nxt = nx_ref[i]

        @pl.when(nxt >= 0)
        def _():
            for cp in fetch(nxt, 1 - slot):
                cp.start()

        wgu_s[...] = wgu_f[slot].astype(BF16)
        wdn_s[...] = wdn_f[slot].astype(BF16)

    @pl.when(i < nu_ref[0])
    def _():
        p = xs_ref[...]
        xa = pltpu.bitcast(p & jnp.uint32(0xFFFF0000), F32).astype(BF16)
        xb = pltpu.bitcast(p << 16, F32).astype(BF16)
        gu = _dot(jnp.concatenate([xa, xb], axis=1), wgu_s[...])
        act = (_silu(gu[:, :D_EXPERT]) * gu[:, D_EXPERT:]).astype(BF16)
        o_ref[...] = _dot(act, wdn_s[...])

    @pl.when(i >= nu_ref[0])
    def _():
        o_ref[...] = jnp.zeros(o_ref.shape, F32)


def _experts(blk_exp, n_used, xs, w_gu, w_dn, layer, tb):
    n_slots = xs.shape[0]
    n_blocks = n_slots // tb
    half = D_MODEL // 2
    ids = jnp.arange(n_blocks, dtype=I32)
    change = jnp.concatenate([jnp.zeros((1,), bool), blk_exp[1:] != blk_exp[:-1]])
    nxt_idx = lax.cummin(jnp.where(change, ids, n_blocks), reverse=True)
    nxt_idx = jnp.concatenate([nxt_idx[1:], jnp.full((1,), n_blocks, I32)])
    nxt_exp = jnp.where(nxt_idx < n_blocks, blk_exp[jnp.minimum(nxt_idx, n_blocks - 1)], -1).astype(I32)
    grid_spec = pltpu.PrefetchScalarGridSpec(
        num_scalar_prefetch=3,
        grid=(n_blocks,),
        in_specs=[pl.BlockSpec((tb, half), lambda i, be, nx, nu: (i, 0)),
                  pl.BlockSpec(memory_space=pl.ANY), pl.BlockSpec(memory_space=pl.ANY)],
        out_specs=pl.BlockSpec((tb, D_MODEL), lambda i, be, nx, nu: (i, 0)),
        scratch_shapes=[pltpu.VMEM((2, D_MODEL, 2 * D_EXPERT), F32), pltpu.VMEM((2, D_EXPERT, D_MODEL), F32),
                        pltpu.VMEM((D_MODEL, 2 * D_EXPERT), BF16), pltpu.VMEM((D_EXPERT, D_MODEL), BF16),
                        pltpu.SMEM((1,), I32), pltpu.SemaphoreType.DMA((2, 2))],
    )
    return pl.pallas_call(
        functools.partial(_expert_kernel, layer=layer),
        grid_spec=grid_spec,
        out_shape=jax.ShapeDtypeStruct((n_slots, D_MODEL), F32),
        compiler_params=_cparams(("arbitrary",)),
        name="moe_experts",
    )(blk_exp, nxt_exp, n_used, xs, w_gu, w_dn)


def _combine_kernel(dest_ref, destn_ref, x_ref, rt_ref, pe_ref, g_ref, wpg_ref, bpg_ref, wple_ref, outs_ref, o_ref,
                    g_s, x_s, hn_s, pe_s, sem):
    i = pl.program_id(0)
    nt = pl.num_programs(0)
    tm = x_ref.shape[0]

    nsl = 8
    sw = D_MODEL // nsl
    per = tm // nsl

    def gather(d_ref, buf, lo, hi):
        for j in range(lo, hi):
            for kk in range(2):
                _row_copy(outs_ref, d_ref[0, 0, kk * tm + j], g_s.at[buf, kk], j, sem.at[buf]).start()

    def drain(buf):
        for _ in range(2 * tm):
            _row_copy(outs_ref, 0, g_s.at[buf, 0], 0, sem.at[buf]).wait()

    @pl.when(i == 0)
    def _():
        gather(dest_ref, 0, 0, tm)

    def step(cur):
        drain(cur)
        rt = rt_ref[...]
        x = x_ref[...] + (g_s[cur, 0] * rt[:, 2:3] + g_s[cur, 1] * rt[:, 3:4])
        x_s[...] = x
        hn_s[...] = _rms(x, g_ref[...]).astype(BF16)
        pe_s[...] = pe_ref[...].astype(BF16)
        for c in range(nsl):
            gather(destn_ref, 1 - cur, c * per, (c + 1) * per)
            cols = slice(c * sw, (c + 1) * sw)
            gate = jax.nn.sigmoid(_dot(hn_s[...], wpg_ref[:, cols]) + bpg_ref[:, cols])
            o_ref[:, cols] = x_s[:, cols] + gate * _dot(pe_s[...], wple_ref[:, cols])

        @pl.when(i == nt - 1)
        def _():
            drain(1 - cur)

    for par in range(2):
        pl.when(i % 2 == par)(functools.partial(step, par))


def _combine(dest, x, rt, pe, outs, lw, tm):
    n = x.shape[0]
    nt = n // tm
    row = lambda w: pl.BlockSpec((tm, w), lambda i: (i, 0))
    return pl.pallas_call(
        _combine_kernel,
        grid=(nt,),
        in_specs=[pl.BlockSpec((1, 1, 2 * tm), lambda i: (i, 0, 0), memory_space=pltpu.SMEM),
                  pl.BlockSpec((1, 1, 2 * tm), lambda i: (jnp.minimum(i + 1, nt - 1), 0, 0), memory_space=pltpu.SMEM),
                  row(D_MODEL), row(LANES), row(D_PLE), _const_spec((1, D_MODEL)),
                  _const_spec((D_MODEL, D_MODEL)), _const_spec((1, D_MODEL)), _const_spec((D_PLE, D_MODEL)),
                  pl.BlockSpec(memory_space=pl.ANY)],
        out_specs=row(D_MODEL),
        out_shape=jax.ShapeDtypeStruct((n, D_MODEL), F32),
        scratch_shapes=[pltpu.VMEM((2, 2, tm, D_MODEL), F32), pltpu.VMEM((tm, D_MODEL), F32),
                        pltpu.VMEM((tm, D_MODEL), BF16), pltpu.VMEM((tm, D_PLE), BF16), pltpu.SemaphoreType.DMA((2,))],
        compiler_params=_cparams(("arbitrary",)),
        name="moe_combine_ple",
    )(dest, dest, x, rt, pe, lw["g_ple"], lw["w_pg"], lw["b_pg"], lw["w_ple"], outs)


def _tile_dest(dest, tm):
    n = dest.shape[1]
    return dest.reshape(2, n // tm, tm).transpose(1, 0, 2).reshape(n // tm, 1, 2 * tm)


def _moe_ple(x, hfp, rt, rtt, pe, lw, slots):
    n = x.shape[0]
    tb = min(256, n)
    rank, cnt = _rank(rtt)
    counts = cnt[:, 0].astype(I32)
    padded = (counts + tb - 1) // tb * tb
    pend = jnp.cumsum(padded)
    pstart = pend - padded
    e = rtt[0:2].astype(I32)
    eids = jnp.arange(N_EXPERTS, dtype=I32)
    start_e = jnp.sum(jnp.where(e[..., None] == eids, pstart, 0), axis=-1)
    dest = start_e + rank[0:2].astype(I32)
    n_blocks = -(-2 * n // tb) + N_EXPERTS
    blk_start = jnp.arange(n_blocks, dtype=I32) * tb
    blk_exp = jnp.minimum(jnp.sum((pend[None, :] <= blk_start[:, None]).astype(I32), axis=1), N_EXPERTS - 1)
    n_used = (pend[-1:] // tb).astype(I32)
    td = min(512, n)
    if slots is None:
        slots = jnp.zeros((n_blocks * tb, D_MODEL // 2), U32)
    xs = _dispatch(_tile_dest(dest, td), hfp, slots, td)
    outs = _experts(blk_exp, n_used, xs, lw["w_gate_up"], lw["w_down"], lw["layer"], tb)
    tc = min(256, n)
    return _combine(_tile_dest(dest, tc), x, rt, pe, outs, lw, tc), xs


def _layer(x, pe, lw, cos, sin, lam_init, slots=None):
    b, seq, _ = x.shape
    n = b * seq
    pool_u, q, k, v, z, xbc, dt = _inproj(x.reshape(n, D_MODEL), lw, cos, sin, seq)
    sh = lambda a: a.reshape(b, seq, a.shape[-1])
    y_pool = _pool(sh(pool_u), lw)
    y_da = _attention(sh(q), sh(k), sh(v), lw["lam"], lw["subln_g"], lam_init)
    y_ssd = _ssd(sh(xbc), sh(dt), sh(z), lw)
    x2, hfp, rt, rtt = _outproj(x.reshape(n, D_MODEL), y_pool.reshape(n, D_POOL), y_da.reshape(n, D_DA),
                                y_ssd.reshape(n, D_SSD), lw)
    x3, slots = _moe_ple(x2, hfp, rt, rtt, pe.reshape(n, D_PLE), lw, slots)
    return x3.reshape(b, seq, D_MODEL), slots


def _rope_tables(seq):
    half = DA_DH // 2
    inv = jnp.power(ROPE_THETA, -jnp.arange(half, dtype=F32) * 2.0 / DA_DH)
    ang = jnp.arange(seq, dtype=F32)[:, None] * inv[None, :]
    cos = jnp.cos(ang)
    sin = jnp.sin(ang)
    return jnp.tile(cos, (1, 4)), jnp.tile(jnp.concatenate([-sin, sin], axis=1), (1, 2))


def _prep_layer(i, w):
    row = lambda a: a.reshape(1, -1).astype(F32)
    w_in = w["w_in"][i].astype(BF16)
    o = 0
    lw = {"g_mix": row(w["norm_mix_g"][i])}
    for name, width in (("w_pool", D_POOL), ("w_q", D_DA), ("w_k", D_DA), ("w_v", D_DA), ("w_z", D_SSD), ("w_xbc", D_XBC)):
        lw[name] = w_in[:, o:o + width]
        o += width
    lw["w_dt"] = jnp.pad(w_in[:, o:], ((0, 0), (0, LANES - 2 * SSD_HEADS)))
    lw["gq"] = row(jnp.tile(w["q_norm_g"][i], 2 * DA_HEADS)) * (DA_DH ** -0.5 * math.log2(math.e))
    lw["gk"] = row(jnp.tile(w["k_norm_g"][i], 2 * DA_HEADS))
    lw["pool_w"] = w["pool_w"][i].astype(BF16)
    lw["pool_scale"] = row(w["pool_scale"][i])
    lf = w["lam_qk"][i].astype(F32)
    lam_init = 0.8 - 0.6 * math.exp(-0.3 * i)
    lam = jnp.exp(jnp.sum(lf[0] * lf[1])) - jnp.exp(jnp.sum(lf[2] * lf[3])) + lam_init
    lw["lam"] = jnp.full((1, DA_DV), lam, F32)
    lw["subln_g"] = row(w["subln_g"][i])
    lw["conv_w"] = jnp.pad(w["conv_w"][i].astype(F32), ((0, 8 - CONV_K), (0, 0)))
    lw["conv_b"] = row(w["conv_b"][i])
    pad_row = lambda a: jnp.pad(a.reshape(1, -1).astype(F32), ((0, 0), (0, LANES - 2 * SSD_HEADS)))
    lw["dt_bias"] = pad_row(w["dt_bias"][i])
    lw["a_log"] = pad_row(w["a_log"][i])
    lw["d_skip"] = row(jnp.repeat(w["d_skip"][i], SSD_HEAD_DIM))
    lw["ssd_norm_g"] = row(w["ssd_norm_g"][i])
    w_out = w["w_out"][i].astype(BF16)
    lw["wo_pool"] = w_out[:D_POOL]
    lw["wo_da"] = w_out[D_POOL:D_POOL + D_DA]
    lw["wo_ssd"] = w_out[D_POOL + D_DA:]
    lw["g_ffn"] = row(w["norm_ffn_g"][i])
    wr = jnp.zeros((D_MODEL, LANES), F32)
    wr = wr.at[:, :N_GROUPS].set(w["router_coarse_w"][i]).at[:, 8:8 + N_EXPERTS].set(w["router_fine_w"][i])
    lw["wr_hi"] = wr.astype(BF16)
    lw["wr_lo"] = (wr - lw["wr_hi"].astype(F32)).astype(BF16)
    rb = jnp.zeros((1, LANES), F32)
    lw["r_bias"] = rb.at[0, :N_GROUPS].set(w["router_coarse_b"][i]).at[0, 8:8 + N_EXPERTS].set(w["router_fine_b"][i])
    lw["w_gate_up"] = w["w_gate_up"]
    lw["w_down"] = w["w_down"]
    lw["layer"] = i
    lw["g_ple"] = row(w["norm_ple_g"][i])
    lw["w_pg"] = w["w_ple_gate"][i].astype(BF16)
    lw["b_pg"] = row(w["b_ple_gate"][i])
    lw["w_ple"] = w["w_ple"][i].astype(BF16)
    return lw, lam_init


def kernel(x_prompt, x_sample, p_prompt, p_sample, w_in, w_out, pool_w, pool_scale, q_norm_g, k_norm_g, lam_qk, subln_g, conv_w, conv_b, a_log, dt_bias, d_skip, ssd_norm_g, norm_mix_g, norm_ffn_g, router_coarse_w, router_coarse_b, router_fine_w, router_fine_b, w_gate_up, w_down, norm_ple_g, w_ple, w_ple_gate, b_ple_gate):
    w = dict(w_in=w_in, w_out=w_out, pool_w=pool_w, pool_scale=pool_scale, q_norm_g=q_norm_g, k_norm_g=k_norm_g,
             lam_qk=lam_qk, subln_g=subln_g, conv_w=conv_w, conv_b=conv_b, a_log=a_log, dt_bias=dt_bias,
             d_skip=d_skip, ssd_norm_g=ssd_norm_g, norm_mix_g=norm_mix_g, norm_ffn_g=norm_ffn_g,
             router_coarse_w=router_coarse_w, router_coarse_b=router_coarse_b, router_fine_w=router_fine_w,
             router_fine_b=router_fine_b, w_gate_up=w_gate_up, w_down=w_down, norm_ple_g=norm_ple_g,
             w_ple=w_ple, w_ple_gate=w_ple_gate, b_ple_gate=b_ple_gate)
    depth = w_in.shape[0]
    xs = [x_prompt, x_sample]
    ps = [p_prompt, p_sample]
    tables = [_rope_tables(x.shape[1]) for x in xs]
    slots = [None, None]
    for i in range(depth):
        lw, lam_init = _prep_layer(i, w)
        res = [_layer(x, p[i], lw, cs[0], cs[1], lam_init, sl) for x, p, cs, sl in zip(xs, ps, tables, slots)]
        xs = [r[0] for r in res]
        slots = [r[1] for r in res]
    return tuple(xs)
```
